```python
import jax, jax.numpy as jnp
from jax import lax
import numpy as np

D_MODEL = 2048
BATCH = 2
SEQ = 8192
DEPTH = 4

HEAD_DIM = 128
GRID_W = 64
BLOCK = 128
EPS = 1e-6
NEG_INF = -1e30
NA_HEADS = 8
NA_WIN_R = 8
NA_WIN_C = 16
GQ_HEADS = 8
GKV_HEADS = 2
ROPE_BASE = 10000.0
DIL_GROUPS = ((128, 1), (512, 4), (2048, 16))
DIL_HEADS_PER_GROUP = 4
DIL_HEADS = 12
ALIBI_MAX_EXP = 8.0
WA = NA_HEADS * HEAD_DIM
WB_Q = GQ_HEADS * HEAD_DIM
WB_KV = GKV_HEADS * HEAD_DIM
WC = DIL_HEADS * HEAD_DIM
WC_OUT = DIL_HEADS_PER_GROUP * HEAD_DIM
SPLIT_SIZES = (WA, WA, WA, WB_Q, WB_KV, WB_KV, WC, WC, WC, WA, WB_Q, WC_OUT, 3 * D_MODEL)
N_IN = WA * 4 + WB_Q * 2 + WB_KV * 2 + WC * 3 + WC_OUT + 3 * D_MODEL

kernel_name = "hybrid_gated_parallel_encoder"


def rmsnorm(x, g):
    x32 = x.astype(jnp.float32)
    y = x32 * lax.rsqrt(jnp.mean(x32 * x32, axis=-1, keepdims=True) + EPS)
    return (y * g.astype(jnp.float32)).astype(x.dtype)


def rope_1d(x, pos):
    half = x.shape[-1] // 2
    freqs = ROPE_BASE ** (-jnp.arange(half, dtype=jnp.float32) / half)
    ang = pos.astype(jnp.float32)[:, None] * freqs[None, :]
    cos = jnp.cos(ang)[None, :, None, :].astype(x.dtype)
    sin = jnp.sin(ang)[None, :, None, :].astype(x.dtype)
    x1, x2 = x[..., :half], x[..., half:]
    return jnp.concatenate([x1 * cos - x2 * sin, x1 * sin + x2 * cos], axis=-1)


def axial_rope(x):
    t = jnp.arange(x.shape[1])
    half = x.shape[-1] // 2
    return jnp.concatenate([rope_1d(x[..., :half], t // GRID_W),
                            rope_1d(x[..., half:], t % GRID_W)], axis=-1)


def neighbourhood_attention(q, k, v, rpb):
    bsz, seq, heads, e = q.shape
    rows = seq // GRID_W
    kr = min(NA_WIN_R, rows)
    qg = q.reshape(bsz, rows, GRID_W, heads, e)
    kg = k.reshape(bsz, rows, GRID_W, heads, e)
    vg = v.reshape(bsz, rows, GRID_W, heads, e)
    j = jnp.arange(GRID_W)
    c = jnp.arange(GRID_W)
    cs = jnp.clip(j - NA_WIN_C // 2, 0, GRID_W - NA_WIN_C)
    col_ok = (c[None, :] >= cs[:, None]) & (c[None, :] < cs[:, None] + NA_WIN_C)
    dc_idx = jnp.clip(c[None, :] - j[:, None] + NA_WIN_C - 1, 0, 2 * NA_WIN_C - 2)
    scale = e ** -0.5

    def row_block(r):
        start = jnp.clip(r - kr // 2, 0, rows - kr)
        q_r = lax.dynamic_index_in_dim(qg, r, axis=1, keepdims=False)
        k_r = lax.dynamic_slice_in_dim(kg, start, kr, axis=1)
        v_r = lax.dynamic_slice_in_dim(vg, start, kr, axis=1)
        dr_idx = start + jnp.arange(kr) - r + NA_WIN_R - 1
        bias = rpb[:, dr_idx[None, :, None], dc_idx[:, None, :]]
        s = jnp.einsum('bjhe,bkche->bhjkc', q_r, k_r,
                       preferred_element_type=jnp.float32) * scale + bias.astype(jnp.float32)[None]
        s = jnp.where(col_ok[:, None, :], s, NEG_INF)
        p = jax.nn.softmax(s.reshape(bsz, heads, GRID_W, kr * GRID_W), axis=-1).reshape(s.shape)
        return jnp.einsum('bhjkc,bkche->bjhe', p.astype(v.dtype), v_r)

    o = lax.map(row_block, jnp.arange(rows))
    return o.transpose(1, 0, 2, 3, 4).reshape(bsz, seq, heads * e)


def gqa_attention(q, k, v):
    bsz, seq, hq, e = q.shape
    hkv = k.shape[2]
    grp = hq // hkv
    nb = seq // BLOCK
    scale = e ** -0.5
    qb = q.reshape(bsz, nb, BLOCK, hkv, grp, e).transpose(1, 0, 2, 3, 4, 5)

    def block(qi):
        s = jnp.einsum('bqkge,bske->bkgqs', qi, k, preferred_element_type=jnp.float32) * scale
        p = jax.nn.softmax(s, axis=-1)
        return jnp.einsum('bkgqs,bske->bqkge', p.astype(v.dtype), v)

    o = lax.map(block, qb)
    return o.transpose(1, 0, 2, 3, 4, 5).reshape(bsz, seq, hq * e)


def dilated_group(q, k, v, dil, reach, slopes):
    bsz, seq, hg, e = q.shape
    length = seq // dil
    nb = -(-length // BLOCK)
    lp = nb * BLOCK
    kb_len = BLOCK + 2 * reach
    scale = e ** -0.5

    def to_sub(t):
        return t.reshape(bsz, length, dil, hg, e).transpose(0, 2, 3, 1, 4)

    qs = jnp.pad(to_sub(q), ((0, 0), (0, 0), (0, 0), (0, lp - length), (0, 0)))
    qs = qs.reshape(bsz, dil, hg, nb, BLOCK, e)
    pad_kv = ((0, 0), (0, 0), (0, 0), (reach, lp - length + reach), (0, 0))
    ks = jnp.pad(to_sub(k), pad_kv)
    vs = jnp.pad(to_sub(v), pad_kv)
    idx = jnp.arange(nb)[:, None] * BLOCK + jnp.arange(kb_len)[None, :]
    kbk = ks[:, :, :, idx]
    kbv = vs[:, :, :, idx]
    lq = jnp.arange(nb)[:, None] * BLOCK + jnp.arange(BLOCK)[None, :]
    lk = idx - reach
    dist = jnp.abs(lq[:, :, None] - lk[:, None, :])
    valid = (dist <= reach) & ((lk >= 0) & (lk < length))[:, None, :]
    s = jnp.einsum('bdhnqe,bdhnke->bdhnqk', qs, kbk, preferred_element_type=jnp.float32) * scale
    s = s - slopes[:, None, None, None] * (dil * dist).astype(jnp.float32)
    s = jnp.where(valid, s, NEG_INF)
    lse = jax.nn.logsumexp(s, axis=-1)
    p = jnp.exp(s - lse[..., None])
    o = jnp.einsum('bdhnqk,bdhnke->bdhnqe', p.astype(v.dtype), kbv)
    o = o.reshape(bsz, dil, hg, lp, e)[:, :, :, :length]
    o = o.transpose(0, 3, 1, 2, 4).reshape(bsz, seq, hg, e)
    lse = lse.reshape(bsz, dil, hg, lp)[..., :length].transpose(0, 3, 1, 2).reshape(bsz, seq, hg)
    return o, lse


def dilated_mixture(q, k, v):
    bsz, seq = q.shape[0], q.shape[1]
    slopes = 2.0 ** (-ALIBI_MAX_EXP * jnp.arange(1, DIL_HEADS + 1, dtype=jnp.float32) / DIL_HEADS)
    outs, lses = [], []
    for g, (win, dil) in enumerate(DIL_GROUPS):
        sl = slice(g * DIL_HEADS_PER_GROUP, (g + 1) * DIL_HEADS_PER_GROUP)
        o, l = dilated_group(q[:, :, sl], k[:, :, sl], v[:, :, sl], dil, (win // 2) // dil, slopes[sl])
        outs.append(o)
        lses.append(l)
    wts = jax.nn.softmax(jnp.stack(lses, axis=-1), axis=-1)
    o = jnp.sum(jnp.stack(outs, axis=3) * wts[..., None].astype(q.dtype), axis=3)
    return o.reshape(bsz, seq, WC_OUT)


def setup_inputs(seed: int = 0) -> dict:
    key = jax.random.key(seed)
    ks = jax.random.split(key, 12)
    f32 = jnp.float32
    return {
        "x": jax.random.normal(ks[0], (BATCH, SEQ, D_MODEL), f32),
        "pre_norm_g": 1.0 + 0.02 * jax.random.normal(ks[1], (DEPTH, D_MODEL), f32),
        "w_in": jax.random.normal(ks[2], (DEPTH, D_MODEL, N_IN), f32) * D_MODEL ** -0.5,
        "b_gate": 0.1 * jax.random.normal(ks[3], (DEPTH, 3 * D_MODEL), f32),
        "q_norm_g": 1.0 + 0.02 * jax.random.normal(ks[4], (DEPTH, HEAD_DIM), f32),
        "k_norm_g": 1.0 + 0.02 * jax.random.normal(ks[5], (DEPTH, HEAD_DIM), f32),
        "rpb": 0.1 * jax.random.normal(ks[6], (DEPTH, NA_HEADS, 2 * NA_WIN_R - 1, 2 * NA_WIN_C - 1), f32),
        "w_branch_a": jax.random.normal(ks[7], (DEPTH, WA, D_MODEL), f32) * WA ** -0.5,
        "w_branch_b": jax.random.normal(ks[8], (DEPTH, WB_Q, D_MODEL), f32) * WB_Q ** -0.5,
        "w_branch_c": jax.random.normal(ks[9], (DEPTH, WC_OUT, D_MODEL), f32) * WC_OUT ** -0.5,
        "w_out": jax.random.normal(ks[10], (DEPTH, D_MODEL, D_MODEL), f32) * D_MODEL ** -0.5,
        "post_norm_g": 1.0 + 0.02 * jax.random.normal(ks[11], (DEPTH, D_MODEL), f32),
    }


def reference(x, pre_norm_g, w_in, b_gate, q_norm_g, k_norm_g, rpb, w_branch_a, w_branch_b,
              w_branch_c, w_out, post_norm_g):
    bsz, seq = x.shape[0], x.shape[1]
    split_points = [int(p) for p in np.cumsum(SPLIT_SIZES)[:-1]]

    def heads(t, n):
        return t.reshape(bsz, seq, n, HEAD_DIM)

    for l in range(DEPTH):
        h = rmsnorm(x, pre_norm_g[l])
        proj = jnp.einsum('bsd,dn->bsn', h, w_in[l])
        (qa, ka, va, qb, kb, vb, qc, kc, vc, za, zb, zc, gates) = jnp.split(proj, split_points, axis=-1)
        ya = neighbourhood_attention(heads(qa, NA_HEADS), heads(ka, NA_HEADS), heads(va, NA_HEADS), rpb[l])
        qb_h = axial_rope(rmsnorm(heads(qb, GQ_HEADS), q_norm_g[l]))
        kb_h = axial_rope(rmsnorm(heads(kb, GKV_HEADS), k_norm_g[l]))
        yb = gqa_attention(qb_h, kb_h, heads(vb, GKV_HEADS))
        yc = dilated_mixture(heads(qc, DIL_HEADS), heads(kc, DIL_HEADS), heads(vc, DIL_HEADS))
        g = jax.nn.sigmoid((gates + b_gate[l]).astype(jnp.float32)).astype(x.dtype)
        ga, gb, gc = jnp.split(g, 3, axis=-1)
        merged = (ga * jnp.einsum('bsw,wd->bsd', ya * jax.nn.silu(za), w_branch_a[l])
                  + gb * jnp.einsum('bsw,wd->bsd', yb * jax.nn.silu(zb), w_branch_b[l])
                  + gc * jnp.einsum('bsw,wd->bsd', yc * jax.nn.silu(zc), w_branch_c[l]))
        out = jnp.einsum('bsd,de->bse', merged, w_out[l])
        x = x + rmsnorm(out, post_norm_g[l])
    return x
```

```python
import functools

import numpy as np
import jax
import jax.numpy as jnp
from jax import lax
from jax.experimental import pallas as pl
from jax.experimental.pallas import tpu as pltpu

F32 = jnp.float32
BF16 = jnp.bfloat16

HEAD_DIM = 128
GRID_W = 64
EPS = 1e-6
NEG_INF = -1e30
NA_HEADS = 8
NA_WIN_R = 8
NA_WIN_C = 16
GQ_HEADS = 8
GKV_HEADS = 2
ROPE_BASE = 10000.0
DIL_GROUPS = ((128, 1), (512, 4), (2048, 16))
DIL_HEADS_PER_GROUP = 4
DIL_HEADS = 12
ALIBI_MAX_EXP = 8.0
SCALE = HEAD_DIM ** -0.5

V7X_LANES = 128
V7X_VMEM_BYTES = 64 * 1024 * 1024
MIB = 1024 * 1024

CHUNKS_PER_TILE = 4
N_CHUNKS = 140
N_TILES = N_CHUNKS // CHUNKS_PER_TILE
GATE_TILE0 = 23
C_GATE = 0
C_QA, C_KA, C_VA = 48, 56, 64
C_QB, C_KB, C_VB = 72, 80, 82
C_QC, C_KC, C_VC = 84, 96, 108
C_ZA, C_ZB, C_ZC = 120, 128, 136

NA_R = 4
NA_KW = NA_R + NA_WIN_R - 1
DIL_T = 256


def _cparams(sem, vmem_mib):
    return pltpu.CompilerParams(dimension_semantics=sem, vmem_limit_bytes=int(vmem_mib * MIB))


def _inproj_kernel(x_ref, g_ref, w_ref, p_ref, xn_ref):
    @pl.when(pl.program_id(1) == 0)
    def _():
        x = x_ref[...]
        ms = jnp.mean(x * x, axis=-1, keepdims=True)
        xn_ref[...] = (x * lax.rsqrt(ms + EPS) * g_ref[...]).astype(BF16)

    res = jnp.dot(xn_ref[...], w_ref[...], preferred_element_type=F32)
    for c in range(CHUNKS_PER_TILE):
        p_ref[c] = res[:, c * V7X_LANES:(c + 1) * V7X_LANES].astype(BF16)


def _inproj(x2d, g, w, tm):
    m, d = x2d.shape
    tn = CHUNKS_PER_TILE * V7X_LANES
    return pl.pallas_call(
        _inproj_kernel,
        grid=(m // tm, N_TILES),
        in_specs=[
            pl.BlockSpec((tm, d), lambda i, j: (i, 0)),
            pl.BlockSpec((1, d), lambda i, j: (0, 0)),
            pl.BlockSpec((d, tn), lambda i, j: (0, (j + GATE_TILE0) % N_TILES)),
        ],
        out_specs=pl.BlockSpec((CHUNKS_PER_TILE, tm, V7X_LANES), lambda i, j: (j, i, 0)),
        out_shape=jax.ShapeDtypeStruct((N_CHUNKS, m, V7X_LANES), BF16),
        scratch_shapes=[pltpu.VMEM((tm, d), BF16)],
        compiler_params=_cparams(("parallel", "arbitrary"), 48),
        name="inproj",
    )(x2d, g.reshape(1, d), w)


def _qkprep_kernel(p_ref, gn_ref, cos_ref, sin_ref, o_ref):
    nc, tr, e = p_ref.shape
    lane = lax.broadcasted_iota(jnp.int32, (tr, e), 1)
    first = (lane % (e // 2)) < (e // 4)
    cos = cos_ref[...]
    sin = sin_ref[...]
    gn = gn_ref[0]
    for c in range(nc):
        x = p_ref[c].astype(F32)
        ms = jnp.mean(x * x, axis=-1, keepdims=True)
        y = x * lax.rsqrt(ms + EPS) * gn
        partner = jnp.where(first, pltpu.roll(y, e - e // 4, 1), pltpu.roll(y, e // 4, 1))
        o_ref[c] = (y * cos + partner * sin).astype(BF16)


def _qkprep(p, gains, cos, sin, seq, tr):
    m = p.shape[1]
    nseq = seq // tr
    nblk = (GQ_HEADS + GKV_HEADS) // 2
    return pl.pallas_call(
        _qkprep_kernel,
        grid=(nblk, m // tr),
        in_specs=[
            pl.BlockSpec((2, tr, HEAD_DIM), lambda c, i: (C_QB // 2 + c, i, 0)),
            pl.BlockSpec((1, 1, HEAD_DIM), lambda c, i: (c, 0, 0)),
            pl.BlockSpec((tr, HEAD_DIM), lambda c, i: (i % nseq, 0)),
            pl.BlockSpec((tr, HEAD_DIM), lambda c, i: (i % nseq, 0)),
        ],
        out_specs=pl.BlockSpec((2, tr, HEAD_DIM), lambda c, i: (c, i, 0)),
        out_shape=jax.ShapeDtypeStruct((GQ_HEADS + GKV_HEADS, m, HEAD_DIM), BF16),
        compiler_params=_cparams(("parallel", "parallel"), 32),
        name="qkprep",
    )(p, gains, cos, sin)


def _rope_tables(seq):
    quarter = HEAD_DIM // 4
    freqs = ROPE_BASE ** (-jnp.arange(quarter, dtype=F32) / quarter)
    t = jnp.arange(seq)
    ang_r = (t // GRID_W).astype(F32)[:, None] * freqs[None, :]
    ang_c = (t % GRID_W).astype(F32)[:, None] * freqs[None, :]
    cos = jnp.concatenate([jnp.cos(ang_r), jnp.cos(ang_r), jnp.cos(ang_c), jnp.cos(ang_c)], axis=-1)
    sin = jnp.concatenate([-jnp.sin(ang_r), jnp.sin(ang_r), -jnp.sin(ang_c), jnp.sin(ang_c)], axis=-1)
    return cos, sin


def _gqa_kernel(q_ref, k_ref, v_ref, o_ref, *, tk):
    grp, tq, e = q_ref.shape
    seq = k_ref.shape[1]
    q = q_ref[...].reshape(grp * tq, e)

    def body(i, carry):
        m, l, acc = carry
        start = pl.multiple_of(i * tk, tk)
        k = k_ref[0, pl.ds(start, tk), :]
        v = v_ref[0, pl.ds(start, tk), :]
        s = lax.dot_general(q, k, (((1,), (1,)), ((), ())), preferred_element_type=F32)
        m_new = jnp.maximum(m, jnp.max(s, axis=-1, keepdims=True))
        alpha = jnp.exp(m - m_new)
        p = jnp.exp(s - m_new)
        l = alpha * l + jnp.sum(p, axis=-1, keepdims=True)
        acc = alpha * acc + jnp.dot(p.astype(BF16), v, preferred_element_type=F32)
        return m_new, l, acc

    m0 = jnp.full((grp * tq, 1), -jnp.inf, F32)
    l0 = jnp.zeros((grp * tq, 1), F32)
    a0 = jnp.zeros((grp * tq, e), F32)
    m, l, acc = lax.fori_loop(0, seq // tk, body, (m0, l0, a0))
    o_ref[...] = (acc / l).astype(BF16).reshape(grp, tq, e)


def _gqa(qk, p, bsz, seq, tq, tk):
    m = qk.shape[1]
    grp = GQ_HEADS // GKV_HEADS
    nq = seq // tq
    return pl.pallas_call(
        functools.partial(_gqa_kernel, tk=tk),
        grid=(bsz, GKV_HEADS, nq),
        in_specs=[
            pl.BlockSpec((grp, tq, HEAD_DIM), lambda b, g, i: (g, b * nq + i, 0)),
            pl.BlockSpec((1, seq, HEAD_DIM), lambda b, g, i: (GQ_HEADS + g, b, 0)),
            pl.BlockSpec((1, seq, HEAD_DIM), lambda b, g, i: (C_VB + g, b, 0)),
        ],
        out_specs=pl.BlockSpec((grp, tq, HEAD_DIM), lambda b, g, i: (g, b * nq + i, 0)),
        out_shape=jax.ShapeDtypeStruct((GQ_HEADS, m, HEAD_DIM), BF16),
        compiler_params=_cparams(("parallel", "parallel", "arbitrary"), 48),
        name="gqa",
    )(qk, qk, p)


def _na_kernel(q_ref, k_ref, v_ref, bias_ref, o_ref):
    seq = q_ref.shape[1]
    nq = NA_R * GRID_W
    nkeys = NA_KW * GRID_W
    nblk = seq // nq

    def block(q0, k0, variant):
        q = (q_ref[0, pl.ds(q0, nq), :].astype(F32) * SCALE).astype(BF16)
        k = k_ref[0, pl.ds(k0, nkeys), :]
        v = v_ref[0, pl.ds(k0, nkeys), :]
        s = lax.dot_general(q, k, (((1,), (1,)), ((), ())), preferred_element_type=F32)
        s = s + bias_ref[variant, 0]
        mx = jnp.max(s, axis=-1, keepdims=True)
        p = jnp.exp(s - mx)
        l = jnp.sum(p, axis=-1, keepdims=True)
        o = jnp.dot(p.astype(BF16), v, preferred_element_type=F32)
        o_ref[0, pl.ds(q0, nq), :] = (o / l).astype(BF16)

    block(0, 0, 0)

    def body(b, carry):
        q0 = pl.multiple_of(b * nq, nq)
        k0 = pl.multiple_of(b * nq - (NA_WIN_R // 2) * GRID_W, GRID_W)
        block(q0, k0, 1)
        return carry

    lax.fori_loop(1, nblk - 1, body, 0)
    block(seq - nq, seq - nkeys, 2)


def _na(p, bias, bsz, seq):
    m = p.shape[1]
    nq = NA_R * GRID_W
    nkeys = NA_KW * GRID_W
    return pl.pallas_call(
        _na_kernel,
        grid=(NA_HEADS, bsz),
        in_specs=[
            pl.BlockSpec((1, seq, HEAD_DIM), lambda h, b: (C_QA + h, b, 0)),
            pl.BlockSpec((1, seq, HEAD_DIM), lambda h, b: (C_KA + h, b, 0)),
            pl.BlockSpec((1, seq, HEAD_DIM), lambda h, b: (C_VA + h, b, 0)),
            pl.BlockSpec((3, 1, nq, nkeys), lambda h, b: (0, h, 0, 0)),
        ],
        out_specs=pl.BlockSpec((1, seq, HEAD_DIM), lambda h, b: (h, b, 0)),
        out_shape=jax.ShapeDtypeStruct((NA_HEADS, m, HEAD_DIM), BF16),
        compiler_params=_cparams(("parallel", "parallel"), 48),
        name="na",
    )(p, p, p, bias)


def _na_bias(rpb_l, seq):
    rows = seq // GRID_W
    kr = min(NA_WIN_R, rows)
    ri = np.arange(NA_R)[:, None, None, None]
    j = np.arange(GRID_W)[None, :, None, None]
    kk = np.arange(NA_KW)[None, None, :, None]
    c = np.arange(GRID_W)[None, None, None, :]
    cs = np.clip(j - NA_WIN_C // 2, 0, GRID_W - NA_WIN_C)
    col_ok = (c >= cs) & (c < cs + NA_WIN_C)
    dc = np.clip(c - j + NA_WIN_C - 1, 0, 2 * NA_WIN_C - 2)
    out = []
    for r0, k0 in ((0, 0), (NA_R, NA_R - kr // 2), (rows - NA_R, rows - NA_KW)):
        r = r0 + ri
        krow = k0 + kk
        start = np.clip(r - kr // 2, 0, rows - kr)
        row_ok = (krow >= start) & (krow < start + kr)
        dr = np.clip(krow - r + NA_WIN_R - 1, 0, 2 * NA_WIN_R - 2)
        ok = np.broadcast_to(row_ok & col_ok, (NA_R, GRID_W, NA_KW, GRID_W))
        dr_b = np.broadcast_to(dr, ok.shape).reshape(NA_R * GRID_W, NA_KW * GRID_W)
        dc_b = np.broadcast_to(dc, ok.shape).reshape(NA_R * GRID_W, NA_KW * GRID_W)
        ok = ok.reshape(NA_R * GRID_W, NA_KW * GRID_W)
        b = rpb_l[:, dr_b, dc_b].astype(F32)
        out.append(jnp.where(ok[None], b, NEG_INF))
    return jnp.stack(out, axis=0)


def _dil_chunks():
    out = []
    for g, (win, dil) in enumerate(DIL_GROUPS):
        span = win // 2
        nside = -(-span // DIL_T)
        for o in range(-nside, nside + 1):
            out.append((g, o * DIL_T))
    return out


def _dil_kernel(q0_ref, q1_ref, q2_ref, k0_ref, k1_ref, k2_ref, v0_ref, v1_ref, v2_ref, t_ref, o_ref,
                m_sc, l_sc, acc_sc):
    q_refs = (q0_ref, q1_ref, q2_ref)
    k_refs = (k0_ref, k1_ref, k2_ref)
    v_refs = (v0_ref, v1_ref, v2_ref)
    seq = k0_ref.shape[1]
    t0 = pl.program_id(2) * DIL_T
    m_sc[...] = jnp.full(m_sc.shape, -jnp.inf, F32)
    l_sc[...] = jnp.zeros(l_sc.shape, F32)
    acc_sc[...] = jnp.zeros(acc_sc.shape, F32)
    qs = [(r[0].astype(F32) * SCALE).astype(BF16) for r in q_refs]

    for idx, (g, off) in enumerate(_dil_chunks()):
        c0 = t0 + off

        def update(g=g, idx=idx, c0=c0):
            start = pl.multiple_of(c0, DIL_T)
            k = k_refs[g][0, pl.ds(start, DIL_T), :]
            v = v_refs[g][0, pl.ds(start, DIL_T), :]
            s = lax.dot_general(qs[g], k, (((1,), (1,)), ((), ())), preferred_element_type=F32)
            s = s + t_ref[0, idx]
            m = m_sc[...]
            m_new = jnp.maximum(m, jnp.max(s, axis=-1, keepdims=True))
            alpha = jnp.exp(m - m_new)
            p = jnp.exp(s - m_new)
            l_sc[...] = alpha * l_sc[...] + jnp.sum(p, axis=-1, keepdims=True)
            acc_sc[...] = alpha * acc_sc[...] + jnp.dot(p.astype(BF16), v, preferred_element_type=F32)
            m_sc[...] = m_new

        if off == 0:
            update()
        else:
            pl.when((c0 >= 0) & (c0 < seq))(update)

    o_ref[0] = (acc_sc[...] / l_sc[...]).astype(BF16)


def _dil(p, table, bsz, seq):
    m = p.shape[1]
    nq = seq // DIL_T
    nchunk = table.shape[1]
    hg = DIL_HEADS_PER_GROUP

    def qspec(g):
        return pl.BlockSpec((1, DIL_T, HEAD_DIM), lambda b, h, i: (C_QC + hg * g + h, b * nq + i, 0))

    def kvspec(base, g):
        return pl.BlockSpec((1, seq, HEAD_DIM), lambda b, h, i: (base + hg * g + h, b, 0))

    return pl.pallas_call(
        _dil_kernel,
        grid=(bsz, hg, nq),
        in_specs=[qspec(0), qspec(1), qspec(2),
                  kvspec(C_KC, 0), kvspec(C_KC, 1), kvspec(C_KC, 2),
                  kvspec(C_VC, 0), kvspec(C_VC, 1), kvspec(C_VC, 2),
                  pl.BlockSpec((1, nchunk, DIL_T, DIL_T), lambda b, h, i: (h, 0, 0, 0))],
        out_specs=pl.BlockSpec((1, DIL_T, HEAD_DIM), lambda b, h, i: (h, b * nq + i, 0)),
        out_shape=jax.ShapeDtypeStruct((hg, m, HEAD_DIM), BF16),
        scratch_shapes=[pltpu.VMEM((DIL_T, 1), F32), pltpu.VMEM((DIL_T, 1), F32),
                        pltpu.VMEM((DIL_T, HEAD_DIM), F32)],
        compiler_params=_cparams(("parallel", "parallel", "arbitrary"), 48),
        name="dilated",
    )(p, p, p, p, p, p, p, p, p, table)


def _dil_table():
    slopes = 2.0 ** (-ALIBI_MAX_EXP * jnp.arange(1, DIL_HEADS + 1, dtype=F32) / DIL_HEADS)
    i = np.arange(DIL_T)[:, None]
    j = np.arange(DIL_T)[None, :]
    per_head = []
    for h in range(DIL_HEADS_PER_GROUP):
        tiles = []
        for g, off in _dil_chunks():
            win, dil = DIL_GROUPS[g]
            rel = i - j - off
            ok = (np.abs(rel) <= (win // 2)) & (rel % dil == 0)
            pen = -slopes[DIL_HEADS_PER_GROUP * g + h] * jnp.asarray(np.abs(rel), F32)
            tiles.append(jnp.where(ok, pen, NEG_INF))
        per_head.append(jnp.stack(tiles, axis=0))
    return jnp.stack(per_head, axis=0)


def _merge_kernel(ya_ref, yb_ref, yc_ref, za_ref, zb_ref, zc_ref, ga_ref, gb_ref, gc_ref, bg_ref, x_ref,
                  wa_ref, wb_ref, wc_ref, wo_ref, pg_ref, o_ref):
    d = x_ref.shape[1]

    def cat(ref):
        return jnp.concatenate([ref[c] for c in range(ref.shape[0])], axis=-1).astype(F32)

    def branch(y_ref, z_ref, w_ref):
        z = cat(z_ref)
        u = (cat(y_ref) * (z * jax.nn.sigmoid(z))).astype(BF16)
        return jnp.dot(u, w_ref[...], preferred_element_type=F32)

    def gate(g_ref, k):
        return jax.nn.sigmoid(cat(g_ref) + bg_ref[:, k * d:(k + 1) * d])

    merged = (gate(ga_ref, 0) * branch(ya_ref, za_ref, wa_ref)
              + gate(gb_ref, 1) * branch(yb_ref, zb_ref, wb_ref)
              + gate(gc_ref, 2) * branch(yc_ref, zc_ref, wc_ref))
    out = jnp.dot(merged.astype(BF16), wo_ref[...], preferred_element_type=F32)
    ms = jnp.mean(out * out, axis=-1, keepdims=True)
    o_ref[...] = x_ref[...] + out * lax.rsqrt(ms + EPS) * pg_ref[...]


def _merge(ya, yb, yc, p, bg, x2d, wa, wb, wc, wo, pg, tm):
    m, d = x2d.shape
    gch = d // V7X_LANES

    def chunks(n, first):
        return pl.BlockSpec((n, tm, HEAD_DIM), lambda i: (first // n, i, 0))

    def resident(shape):
        return pl.BlockSpec(shape, lambda i: (0, 0), pipeline_mode=pl.Buffered(1))

    return pl.pallas_call(
        _merge_kernel,
        grid=(m // tm,),
        in_specs=[
            chunks(NA_HEADS, 0), chunks(GQ_HEADS, 0), chunks(DIL_HEADS_PER_GROUP, 0),
            chunks(NA_HEADS, C_ZA), chunks(GQ_HEADS, C_ZB), chunks(DIL_HEADS_PER_GROUP, C_ZC),
            chunks(gch, C_GATE), chunks(gch, C_GATE + gch), chunks(gch, C_GATE + 2 * gch),
            resident((1, 3 * d)),
            pl.BlockSpec((tm, d), lambda i: (i, 0)),
            resident(wa.shape), resident(wb.shape), resident(wc.shape), resident(wo.shape),
            resident((1, d)),
        ],
        out_specs=pl.BlockSpec((tm, d), lambda i: (i, 0)),
        out_shape=jax.ShapeDtypeStruct((m, d), F32),
        compiler_params=_cparams(("parallel",), 56),
        name="merge",
    )(ya, yb, yc, p, p, p, p, p, p, bg.reshape(1, 3 * d), x2d, wa, wb, wc, wo, pg.reshape(1, d))


def _layer(x2d, bsz, seq, pre_g, w_in, b_gate, q_g, k_g, rpb, wa, wb, wc, wo, post_g, cos, sin, table):
    p = _inproj(x2d, pre_g, w_in.astype(BF16), tm=min(1024, x2d.shape[0]))
    nprep = (GQ_HEADS + GKV_HEADS) // 2
    gains = jnp.concatenate([jnp.broadcast_to(q_g * SCALE, (GQ_HEADS // 2, HEAD_DIM)),
                             jnp.broadcast_to(k_g, (GKV_HEADS // 2, HEAD_DIM))], axis=0)
    qk = _qkprep(p, gains.reshape(nprep, 1, HEAD_DIM), cos, sin, seq, tr=min(1024, seq))
    ya = _na(p, _na_bias(rpb, seq), bsz, seq)
    yb = _gqa(qk, p, bsz, seq, tq=min(256, seq), tk=min(512, seq))
    yc = _dil(p, table, bsz, seq)
    return _merge(ya, yb, yc, p, b_gate, x2d, wa.astype(BF16), wb.astype(BF16), wc.astype(BF16),
                  wo.astype(BF16), post_g, tm=256)


def kernel(x, pre_norm_g, w_in, b_gate, q_norm_g, k_norm_g, rpb, w_branch_a, w_branch_b, w_branch_c, w_out,
           post_norm_g):
    bsz, seq, d = x.shape
    depth = w_in.shape[0]
    cos, sin = _rope_tables(seq)
    table = _dil_table()
    x2d = x.reshape(bsz * seq, d)
    for l in range(depth):
        x2d = _layer(x2d, bsz, seq, pre_norm_g[l], w_in[l], b_gate[l], q_norm_g[l], k_norm_g[l], rpb[l],
                     w_branch_a[l], w_branch_b[l], w_branch_c[l], w_out[l], post_norm_g[l], cos, sin, table)
    return x2d.reshape(bsz, seq, d)
```

```python
import functools
import math

import numpy as np
import jax
import jax.numpy as jnp
from jax import lax
from jax.experimental import pallas as pl
from jax.experimental.pallas import tpu as pltpu

F32 = jnp.float32
BF16 = jnp.bfloat16

HEAD_DIM = 128
GRID_W = 64
EPS = 1e-6
NEG_INF = -1e30
NA_HEADS = 8
NA_WIN_R = 8
NA_WIN_C = 16
GQ_HEADS = 8
GKV_HEADS = 2
ROPE_BASE = 10000.0
DIL_GROUPS = ((128, 1), (512, 4), (2048, 16))
DIL_HEADS_PER_GROUP = 4
DIL_HEADS = 12
ALIBI_MAX_EXP = 8.0
LOG2E = math.log2(math.e)
QSCALE = HEAD_DIM ** -0.5 * LOG2E

V7X_LANES = 128
MIB = 1024 * 1024

CHUNKS_PER_TILE = 4
N_CHUNKS = 140
N_TILES = N_CHUNKS // CHUNKS_PER_TILE
GATE_TILE0 = 23
C_GATE = 0
C_QA, C_KA, C_VA = 48, 56, 64
C_QB, C_KB, C_VB = 72, 80, 82
C_QC, C_KC, C_VC = 84, 96, 108
C_ZA, C_ZB, C_ZC = 120, 128, 136

NA_R = 4
NA_KW = NA_R + NA_WIN_R - 1
DIL_T = 256
DIL_Q = 4 * DIL_T


def _cparams(sem, vmem_mib):
    return pltpu.CompilerParams(dimension_semantics=sem, vmem_limit_bytes=int(vmem_mib * MIB))


def _resident(shape, index_map):
    return pl.BlockSpec(shape, index_map, pipeline_mode=pl.Buffered(1))


def _inproj_kernel(x_ref, g_ref, w_ref, p_ref, xn_ref):
    @pl.when(pl.program_id(1) == 0)
    def _():
        x = x_ref[...]
        ms = jnp.mean(x * x, axis=-1, keepdims=True)
        xn_ref[...] = (x * lax.rsqrt(ms + EPS) * g_ref[...]).astype(BF16)

    res = jnp.dot(xn_ref[...], w_ref[...], preferred_element_type=F32)
    for c in range(CHUNKS_PER_TILE):
        p_ref[c] = res[:, c * V7X_LANES:(c + 1) * V7X_LANES].astype(BF16)


def _inproj(x2d, g, w, tm):
    m, d = x2d.shape
    tn = CHUNKS_PER_TILE * V7X_LANES
    return pl.pallas_call(
        _inproj_kernel,
        grid=(m // tm, N_TILES),
        in_specs=[
            pl.BlockSpec((tm, d), lambda i, j: (i, 0)),
            pl.BlockSpec((1, d), lambda i, j: (0, 0)),
            pl.BlockSpec((d, tn), lambda i, j: (0, (j + GATE_TILE0) % N_TILES)),
        ],
        out_specs=pl.BlockSpec((CHUNKS_PER_TILE, tm, V7X_LANES), lambda i, j: (j, i, 0)),
        out_shape=jax.ShapeDtypeStruct((N_CHUNKS, m, V7X_LANES), BF16),
        scratch_shapes=[pltpu.VMEM((tm, d), BF16)],
        compiler_params=_cparams(("parallel", "arbitrary"), 48),
        name="inproj",
    )(x2d, g.reshape(1, d), w)


def _qkprep_kernel(p_ref, gn_ref, cos_ref, sin_ref, o_ref):
    nc, tr, e = p_ref.shape
    lane = lax.broadcasted_iota(jnp.int32, (tr, e), 1)
    first = (lane % (e // 2)) < (e // 4)
    cos = cos_ref[...]
    sin = sin_ref[...]
    gn = gn_ref[0]
    for c in range(nc):
        x = p_ref[c].astype(F32)
        ms = jnp.mean(x * x, axis=-1, keepdims=True)
        y = x * lax.rsqrt(ms + EPS) * gn
        partner = jnp.where(first, pltpu.roll(y, e - e // 4, 1), pltpu.roll(y, e // 4, 1))
        o_ref[c] = (y * cos + partner * sin).astype(BF16)


def _qkprep(p, gains, cos, sin, seq, tr):
    m = p.shape[1]
    nseq = seq // tr
    nblk = (GQ_HEADS + GKV_HEADS) // 2
    return pl.pallas_call(
        _qkprep_kernel,
        grid=(nblk, m // tr),
        in_specs=[
            pl.BlockSpec((2, tr, HEAD_DIM), lambda c, i: (C_QB // 2 + c, i, 0)),
            pl.BlockSpec((1, 1, HEAD_DIM), lambda c, i: (c, 0, 0)),
            pl.BlockSpec((tr, HEAD_DIM), lambda c, i: (i % nseq, 0)),
            pl.BlockSpec((tr, HEAD_DIM), lambda c, i: (i % nseq, 0)),
        ],
        out_specs=pl.BlockSpec((2, tr, HEAD_DIM), lambda c, i: (c, i, 0)),
        out_shape=jax.ShapeDtypeStruct((GQ_HEADS + GKV_HEADS, m, HEAD_DIM), BF16),
        compiler_params=_cparams(("parallel", "parallel"), 32),
        name="qkprep",
    )(p, gains, cos, sin)


def _rope_tables(seq):
    quarter = HEAD_DIM // 4
    freqs = ROPE_BASE ** (-jnp.arange(quarter, dtype=F32) / quarter)
    t = jnp.arange(seq)
    ang_r = (t // GRID_W).astype(F32)[:, None] * freqs[None, :]
    ang_c = (t % GRID_W).astype(F32)[:, None] * freqs[None, :]
    cos = jnp.concatenate([jnp.cos(ang_r), jnp.cos(ang_r), jnp.cos(ang_c), jnp.cos(ang_c)], axis=-1)
    sin = jnp.concatenate([-jnp.sin(ang_r), jnp.sin(ang_r), -jnp.sin(ang_c), jnp.sin(ang_c)], axis=-1)
    return cos, sin


def _gqa_kernel(q_ref, k_ref, v_ref, o_ref, qt_sc, st0_sc, st1_sc, m_sc, l_sc, acc_sc, *, tk):
    grp, tq, e = q_ref.shape
    seq = k_ref.shape[1]
    nq = grp * tq
    nchunks = seq // tk
    qt_sc[...] = q_ref[...].reshape(nq, e).T
    m_sc[...] = jnp.full(m_sc.shape, -jnp.inf, F32)
    l_sc[...] = jnp.zeros(l_sc.shape, F32)
    acc_sc[...] = jnp.zeros(acc_sc.shape, F32)

    def scores(i, dst):
        start = pl.multiple_of(i * tk, tk)
        dst[...] = jnp.dot(k_ref[0, pl.ds(start, tk), :], qt_sc[...], preferred_element_type=F32)

    def update(i, src):
        start = pl.multiple_of(i * tk, tk)
        vt = v_ref[0, pl.ds(start, tk), :].T
        st = src[...]
        m_old = m_sc[...]
        m_new = jnp.maximum(m_old, jnp.max(st, axis=0, keepdims=True))
        alpha = jnp.exp2(m_old - m_new)
        pt = jnp.exp2(st - m_new)
        l_sc[...] = alpha * l_sc[...] + jnp.sum(pt, axis=0, keepdims=True)
        acc_sc[...] = alpha * acc_sc[...] + jnp.dot(vt, pt.astype(BF16), preferred_element_type=F32)
        m_sc[...] = m_new

    scores(0, st0_sc)

    def pair(j, carry):
        scores(2 * j + 1, st1_sc)
        update(2 * j, st0_sc)
        scores(2 * j + 2, st0_sc)
        update(2 * j + 1, st1_sc)
        return carry

    lax.fori_loop(0, nchunks // 2 - 1, pair, 0)
    scores(nchunks - 1, st1_sc)
    update(nchunks - 2, st0_sc)
    update(nchunks - 1, st1_sc)
    o = (acc_sc[...] / l_sc[...]).T
    o_ref[...] = o.astype(BF16).reshape(grp, tq, e)


def _gqa(qk, p, bsz, seq, tq, tk):
    m = qk.shape[1]
    grp = GQ_HEADS // GKV_HEADS
    nq = seq // tq
    return pl.pallas_call(
        functools.partial(_gqa_kernel, tk=tk),
        grid=(bsz, GKV_HEADS, nq),
        in_specs=[
            pl.BlockSpec((grp, tq, HEAD_DIM), lambda b, g, i: (g, b * nq + i, 0)),
            pl.BlockSpec((1, seq, HEAD_DIM), lambda b, g, i: (GQ_HEADS + g, b, 0)),
            pl.BlockSpec((1, seq, HEAD_DIM), lambda b, g, i: (C_VB + g, b, 0)),
        ],
        out_specs=pl.BlockSpec((grp, tq, HEAD_DIM), lambda b, g, i: (g, b * nq + i, 0)),
        out_shape=jax.ShapeDtypeStruct((GQ_HEADS, m, HEAD_DIM), BF16),
        scratch_shapes=[pltpu.VMEM((HEAD_DIM, grp * tq), BF16),
                        pltpu.VMEM((tk, grp * tq), F32), pltpu.VMEM((tk, grp * tq), F32),
                        pltpu.VMEM((1, grp * tq), F32), pltpu.VMEM((1, grp * tq), F32),
                        pltpu.VMEM((HEAD_DIM, grp * tq), F32)],
        compiler_params=_cparams(("parallel", "parallel", "arbitrary"), 48),
        name="gqa",
    )(qk, qk, p)


def _na_kernel(q_ref, k_ref, v_ref, bias_ref, o_ref, sta_sc, stb_sc):
    seq = q_ref.shape[1]
    nq = NA_R * GRID_W
    nkeys = NA_KW * GRID_W
    nblk = seq // nq

    def window(b):
        lead = (NA_WIN_R // 2) * GRID_W
        if isinstance(b, int):
            if b == 0:
                return 0, 0, 0
            if b == nblk - 1:
                return seq - nq, seq - nkeys, 2
            return b * nq, b * nq - lead, 1
        q0 = pl.multiple_of(b * nq, nq)
        return q0, pl.multiple_of(q0 - lead, GRID_W), 1

    def scores(b, dst):
        q0, k0, variant = window(b)
        qt = (q_ref[0, pl.ds(q0, nq), :].astype(F32) * QSCALE).astype(BF16).T
        k = k_ref[0, pl.ds(k0, nkeys), :]
        dst[...] = jnp.dot(k, qt, preferred_element_type=F32) + bias_ref[variant, 0]

    def update(b, src):
        q0, k0, _ = window(b)
        st = src[...]
        vt = v_ref[0, pl.ds(k0, nkeys), :].T
        pt = jnp.exp2(st - jnp.max(st, axis=0, keepdims=True))
        l = jnp.sum(pt, axis=0, keepdims=True)
        ot = jnp.dot(vt, pt.astype(BF16), preferred_element_type=F32)
        o_ref[0, pl.ds(q0, nq), :] = (ot / l).T.astype(BF16)

    scores(0, sta_sc)
    scores(1, stb_sc)
    update(0, sta_sc)

    def pair(j, carry):
        scores(2 * j, sta_sc)
        update(2 * j - 1, stb_sc)
        scores(2 * j + 1, stb_sc)
        update(2 * j, sta_sc)
        return carry

    lax.fori_loop(1, nblk // 2 - 1, pair, 0)
    scores(nblk - 2, sta_sc)
    update(nblk - 3, stb_sc)
    scores(nblk - 1, stb_sc)
    update(nblk - 2, sta_sc)
    update(nblk - 1, stb_sc)


def _na(p, bias, bsz, seq):
    m = p.shape[1]
    nq = NA_R * GRID_W
    nkeys = NA_KW * GRID_W
    return pl.pallas_call(
        _na_kernel,
        grid=(NA_HEADS, bsz),
        in_specs=[
            pl.BlockSpec((1, seq, HEAD_DIM), lambda h, b: (C_QA + h, b, 0)),
            pl.BlockSpec((1, seq, HEAD_DIM), lambda h, b: (C_KA + h, b, 0)),
            pl.BlockSpec((1, seq, HEAD_DIM), lambda h, b: (C_VA + h, b, 0)),
            pl.BlockSpec((3, 1, nkeys, nq), lambda h, b: (0, h, 0, 0)),
        ],
        out_specs=pl.BlockSpec((1, seq, HEAD_DIM), lambda h, b: (h, b, 0)),
        out_shape=jax.ShapeDtypeStruct((NA_HEADS, m, HEAD_DIM), BF16),
        scratch_shapes=[pltpu.VMEM((nkeys, nq), F32), pltpu.VMEM((nkeys, nq), F32)],
        compiler_params=_cparams(("parallel", "parallel"), 48),
        name="na",
    )(p, p, p, bias)


def _na_bias(rpb, seq):
    rows = seq // GRID_W
    kr = min(NA_WIN_R, rows)
    ndr = 2 * NA_WIN_R - 1
    ndc = 2 * NA_WIN_C - 1
    j = np.arange(GRID_W)[:, None]
    c = np.arange(GRID_W)[None, :]
    cs = np.clip(j - NA_WIN_C // 2, 0, GRID_W - NA_WIN_C)
    col_ok = (c >= cs) & (c < cs + NA_WIN_C)
    dc = np.clip(c - j + NA_WIN_C - 1, 0, ndc - 1)
    onehot = (dc[:, :, None] == np.arange(ndc)[None, None, :]).astype(np.float32)
    tiles = jnp.einsum('lhrd,jcd->lhrjc', rpb.astype(F32), jnp.asarray(onehot),
                       precision=lax.Precision.HIGHEST)
    tiles = jnp.where(jnp.asarray(col_ok), tiles * LOG2E, NEG_INF)
    masked = jnp.full(tiles.shape[:2] + (1, GRID_W, GRID_W), NEG_INF, F32)
    tiles = jnp.concatenate([tiles, masked], axis=2)
    variants = []
    for r0, k0 in ((0, 0), (NA_R, NA_R - kr // 2), (rows - NA_R, rows - NA_KW)):
        r = r0 + np.arange(NA_R)[:, None]
        krow = k0 + np.arange(NA_KW)[None, :]
        start = np.clip(r - kr // 2, 0, rows - kr)
        row_ok = (krow >= start) & (krow < start + kr)
        dr = np.where(row_ok, krow - r + NA_WIN_R - 1, ndr)
        sel = jnp.concatenate([tiles[:, :, d:d + 1] for d in dr.T.reshape(-1)], axis=2)
        sel = sel.reshape(sel.shape[:2] + (NA_KW, NA_R, GRID_W, GRID_W))
        sel = sel.transpose(0, 1, 2, 5, 3, 4)
        variants.append(sel.reshape(sel.shape[:2] + (NA_KW * GRID_W, NA_R * GRID_W)))
    return jnp.stack(variants, axis=1)


def _dil_windows(seq):
    out = []
    for win, _ in DIL_GROUPS:
        hw = -(-(win // 2) // DIL_T) * DIL_T
        out.append((hw, min(2 * hw + DIL_T, seq)))
    return out


def _dil_kernel(q0_ref, q1_ref, q2_ref, k0_ref, k1_ref, k2_ref, v0_ref, v1_ref, v2_ref,
                e0_ref, e1_ref, e2_ref, o_ref, sta_sc, stb_sc):
    q_refs = (q0_ref, q1_ref, q2_ref)
    k_refs = (k0_ref, k1_ref, k2_ref)
    v_refs = (v0_ref, v1_ref, v2_ref)
    e_refs = (e0_ref, e1_ref, e2_ref)
    seq = k0_ref.shape[1]
    nsub = q0_ref.shape[1] // DIL_T
    windows = _dil_windows(seq)
    offs = [sum(w for _, w in windows[:g]) for g in range(len(windows))]

    def key_start(sub, g):
        t0 = pl.program_id(2) * q0_ref.shape[1] + sub * DIL_T
        hw, wlen = windows[g]
        ws = pl.multiple_of(jnp.clip(t0 - hw, 0, seq - wlen), DIL_T)
        return ws, pl.multiple_of(wlen - DIL_T - (t0 - ws), DIL_T)

    def scores(sub, dst):
        for g, (hw, wlen) in enumerate(windows):
            ws, erow = key_start(sub, g)
            qt = (q_refs[g][0, pl.ds(sub * DIL_T, DIL_T), :].astype(F32) * QSCALE).astype(BF16).T
            k = k_refs[g][0, pl.ds(ws, wlen), :]
            dst[pl.ds(offs[g], wlen), :] = (jnp.dot(k, qt, preferred_element_type=F32)
                                            + e_refs[g][0, pl.ds(erow, wlen), :])

    def update(sub, src):
        st = src[...]
        pt = jnp.exp2(st - jnp.max(st, axis=0, keepdims=True))
        l = jnp.sum(pt, axis=0, keepdims=True)
        pt = pt.astype(BF16)
        acc = jnp.zeros((HEAD_DIM, DIL_T), F32)
        for g, (hw, wlen) in enumerate(windows):
            ws, _ = key_start(sub, g)
            vt = v_refs[g][0, pl.ds(ws, wlen), :].T
            acc = acc + jnp.dot(vt, pt[offs[g]:offs[g] + wlen], preferred_element_type=F32)
        o_ref[0, pl.ds(sub * DIL_T, DIL_T), :] = (acc / l).T.astype(BF16)

    bufs = (sta_sc, stb_sc)
    scores(0, bufs[0])
    for sub in range(nsub):
        if sub + 1 < nsub:
            scores(sub + 1, bufs[(sub + 1) % 2])
        update(sub, bufs[sub % 2])


def _dil(p, tables, bsz, seq):
    m = p.shape[1]
    tq = min(DIL_Q, seq)
    nq = seq // tq
    hg = DIL_HEADS_PER_GROUP
    nkeys = sum(w for _, w in _dil_windows(seq))

    def qspec(g):
        return pl.BlockSpec((1, tq, HEAD_DIM), lambda b, h, i: (C_QC + hg * g + h, b * nq + i, 0))

    def kvspec(base, g):
        return _resident((1, seq, HEAD_DIM), lambda b, h, i: (base + hg * g + h, b, 0))

    def tspec(t):
        return _resident((1,) + t.shape[1:], lambda b, h, i: (h, 0, 0))

    return pl.pallas_call(
        _dil_kernel,
        grid=(bsz, hg, nq),
        in_specs=[qspec(0), qspec(1), qspec(2),
                  kvspec(C_KC, 0), kvspec(C_KC, 1), kvspec(C_KC, 2),
                  kvspec(C_VC, 0), kvspec(C_VC, 1), kvspec(C_VC, 2),
                  tspec(tables[0]), tspec(tables[1]), tspec(tables[2])],
        out_specs=pl.BlockSpec((1, tq, HEAD_DIM), lambda b, h, i: (h, b * nq + i, 0)),
        out_shape=jax.ShapeDtypeStruct((hg, m, HEAD_DIM), BF16),
        scratch_shapes=[pltpu.VMEM((nkeys, DIL_T), F32), pltpu.VMEM((nkeys, DIL_T), F32)],
        compiler_params=_cparams(("parallel", "parallel", "arbitrary"), 56),
        name="dilated",
    )(p, p, p, p, p, p, p, p, p, *tables)


def _dil_tables(seq):
    slopes = 2.0 ** (-ALIBI_MAX_EXP * jnp.arange(1, DIL_HEADS + 1, dtype=F32) / DIL_HEADS)
    out = []
    for g, (hw, wlen) in enumerate(_dil_windows(seq)):
        win, dil = DIL_GROUPS[g]
        r = np.arange(2 * wlen - DIL_T)[:, None]
        i = np.arange(DIL_T)[None, :]
        rel = i - r + wlen - DIL_T
        ok = jnp.asarray((np.abs(rel) <= win // 2) & (rel % dil == 0))
        dist = jnp.asarray(np.abs(rel), F32)
        sl = slopes[DIL_HEADS_PER_GROUP * g:DIL_HEADS_PER_GROUP * (g + 1)] * LOG2E
        out.append(jnp.where(ok[None], -sl[:, None, None] * dist[None], NEG_INF))
    return out


def _merge_kernel(ya_ref, yb_ref, yc_ref, za_ref, zb_ref, zc_ref, ga_ref, gb_ref, gc_ref, bg_ref, x_ref,
                  wa_ref, wb_ref, wc_ref, wo_ref, pg_ref, o_ref):
    d = x_ref.shape[1]

    def cat(ref):
        return jnp.concatenate([ref[c] for c in range(ref.shape[0])], axis=-1).astype(F32)

    def branch(y_ref, z_ref, w_ref):
        z = cat(z_ref)
        u = (cat(y_ref) * (z * jax.nn.sigmoid(z))).astype(BF16)
        return jnp.dot(u, w_ref[...], preferred_element_type=F32)

    def gate(g_ref, k):
        return jax.nn.sigmoid(cat(g_ref) + bg_ref[:, k * d:(k + 1) * d])

    merged = (gate(ga_ref, 0) * branch(ya_ref, za_ref, wa_ref)
              + gate(gb_ref, 1) * branch(yb_ref, zb_ref, wb_ref)
              + gate(gc_ref, 2) * branch(yc_ref, zc_ref, wc_ref))
    out = jnp.dot(merged.astype(BF16), wo_ref[...], preferred_element_type=F32)
    ms = jnp.mean(out * out, axis=-1, keepdims=True)
    o_ref[...] = x_ref[...] + out * lax.rsqrt(ms + EPS) * pg_ref[...]


def _merge(ya, yb, yc, p, bg, x2d, wa, wb, wc, wo, pg, tm):
    m, d = x2d.shape
    gch = d // V7X_LANES

    def chunks(n, first):
        return pl.BlockSpec((n, tm, HEAD_DIM), lambda i: (first // n, i, 0))

    def const(shape):
        return _resident(shape, lambda i: (0, 0))

    return pl.pallas_call(
        _merge_kernel,
        grid=(m // tm,),
        in_specs=[
            chunks(NA_HEADS, 0), chunks(GQ_HEADS, 0), chunks(DIL_HEADS_PER_GROUP, 0),
            chunks(NA_HEADS, C_ZA), chunks(GQ_HEADS, C_ZB), chunks(DIL_HEADS_PER_GROUP, C_ZC),
            chunks(gch, C_GATE), chunks(gch, C_GATE + gch), chunks(gch, C_GATE + 2 * gch),
            const((1, 3 * d)),
            pl.BlockSpec((tm, d), lambda i: (i, 0)),
            const(wa.shape), const(wb.shape), const(wc.shape), const(wo.shape),
            const((1, d)),
        ],
        out_specs=pl.BlockSpec((tm, d), lambda i: (i, 0)),
        out_shape=jax.ShapeDtypeStruct((m, d), F32),
        compiler_params=_cparams(("parallel",), 56),
        name="merge",
    )(ya, yb, yc, p, p, p, p, p, p, bg.reshape(1, 3 * d), x2d, wa, wb, wc, wo, pg.reshape(1, d))


def _layer(x2d, bsz, seq, pre_g, w_in, b_gate, q_g, k_g, na_bias, wa, wb, wc, wo, post_g, cos, sin, tables):
    p = _inproj(x2d, pre_g, w_in, tm=min(1024, x2d.shape[0]))
    nprep = (GQ_HEADS + GKV_HEADS) // 2
    gains = jnp.concatenate([jnp.broadcast_to(q_g * QSCALE, (GQ_HEADS // 2, HEAD_DIM)),
                             jnp.broadcast_to(k_g, (GKV_HEADS // 2, HEAD_DIM))], axis=0)
    qk = _qkprep(p, gains.reshape(nprep, 1, HEAD_DIM), cos, sin, seq, tr=min(1024, seq))
    ya = _na(p, na_bias, bsz, seq)
    yb = _gqa(qk, p, bsz, seq, tq=min(256, seq), tk=min(512, seq))
    yc = _dil(p, tables, bsz, seq)
    return _merge(ya, yb, yc, p, b_gate, x2d, wa, wb, wc, wo, post_g, tm=256)


def kernel(x, pre_norm_g, w_in, b_gate, q_norm_g, k_norm_g, rpb, w_branch_a, w_branch_b, w_branch_c, w_out,
           post_norm_g):
    bsz, seq, d = x.shape
    depth = w_in.shape[0]
    cos, sin = _rope_tables(seq)
    tables = _dil_tables(seq)
    na_bias = _na_bias(rpb, seq)
    w_in, wa, wb, wc, wo = (w.astype(BF16) for w in (w_in, w_branch_a, w_branch_b, w_branch_c, w_out))
    x2d = x.reshape(bsz * seq, d)
    for l in range(depth):
        x2d = _layer(x2d, bsz, seq, pre_norm_g[l], w_in[l], b_gate[l], q_norm_g[l], k_norm_g[l], na_bias[l],
                     wa[l], wb[l], wc[l], wo[l], post_norm_g[l], cos, sin, tables)
    return x2d.reshape(bsz, seq, d)
```

```python
import functools
import math

import numpy as np
import jax
import jax.numpy as jnp
from jax import lax
from jax.experimental import pallas as pl
from jax.experimental.pallas import tpu as pltpu

F32 = jnp.float32
BF16 = jnp.bfloat16

HEAD_DIM = 128
GRID_W = 64
EPS = 1e-6
NEG_INF = -1e30
NA_HEADS = 8
NA_WIN_R = 8
NA_WIN_C = 16
GQ_HEADS = 8
GKV_HEADS = 2
ROPE_BASE = 10000.0
DIL_GROUPS = ((128, 1), (512, 4), (2048, 16))
DIL_HEADS_PER_GROUP = 4
DIL_HEADS = 12
ALIBI_MAX_EXP = 8.0
LOG2E = math.log2(math.e)
QSCALE = HEAD_DIM ** -0.5 * LOG2E

V7X_LANES = 128
MIB = 1024 * 1024

CHUNKS_PER_TILE = 20
SUB_CHUNKS = 4
N_CHUNKS = 140
N_TILES = N_CHUNKS // CHUNKS_PER_TILE
TILE_ROT = 3
C_VC = 0
C_ZA, C_ZB, C_ZC = 12, 20, 28
C_GATE = 32
C_QA, C_KA, C_VA = 80, 88, 96
C_QB, C_KB, C_VB = 104, 112, 114
C_QC, C_KC = 116, 128

NA_R = 4
NA_KW = NA_R + NA_WIN_R - 1
GQA_UNROLL = 4
DIL_T = 256
DIL_Q = 4 * DIL_T


def _cparams(sem, vmem_mib):
    return pltpu.CompilerParams(dimension_semantics=sem, vmem_limit_bytes=int(vmem_mib * MIB))


def _resident(shape, index_map):
    return pl.BlockSpec(shape, index_map, pipeline_mode=pl.Buffered(1))


def _inproj_kernel(x_ref, g_ref, w_ref, p_ref, xn_ref):
    @pl.when(pl.program_id(1) == 0)
    def _():
        x = x_ref[...]
        ms = jnp.mean(x * x, axis=-1, keepdims=True)
        xn_ref[...] = (x * lax.rsqrt(ms + EPS) * g_ref[...]).astype(BF16)

    sub_w = SUB_CHUNKS * V7X_LANES
    for s in range(CHUNKS_PER_TILE // SUB_CHUNKS):
        res = jnp.dot(xn_ref[...], w_ref[:, s * sub_w:(s + 1) * sub_w], preferred_element_type=F32)
        for c in range(SUB_CHUNKS):
            p_ref[s * SUB_CHUNKS + c] = res[:, c * V7X_LANES:(c + 1) * V7X_LANES].astype(BF16)


def _inproj(x2d, g, w, layer, tm):
    m, d = x2d.shape
    tn = CHUNKS_PER_TILE * V7X_LANES
    return pl.pallas_call(
        _inproj_kernel,
        grid=(m // tm, N_TILES),
        in_specs=[
            pl.BlockSpec((tm, d), lambda i, j: (i, 0)),
            pl.BlockSpec((1, d), lambda i, j: (0, 0)),
            pl.BlockSpec((None, d, tn), lambda i, j: (layer, 0, (j + TILE_ROT) % N_TILES)),
        ],
        out_specs=pl.BlockSpec((CHUNKS_PER_TILE, tm, V7X_LANES), lambda i, j: (j, i, 0)),
        out_shape=jax.ShapeDtypeStruct((N_CHUNKS, m, V7X_LANES), BF16),
        scratch_shapes=[pltpu.VMEM((tm, d), BF16)],
        compiler_params=_cparams(("parallel", "arbitrary"), 58),
        name="inproj",
    )(x2d, g.reshape(1, d), w)


def _qkprep_kernel(p_ref, gn_ref, cos_ref, sin_ref, o_ref):
    nc, tr, e = p_ref.shape
    lane = lax.broadcasted_iota(jnp.int32, (tr, e), 1)
    first = (lane % (e // 2)) < (e // 4)
    cos = cos_ref[...]
    sin = sin_ref[...]
    gn = gn_ref[0]
    for c in range(nc):
        x = p_ref[c].astype(F32)
        ms = jnp.mean(x * x, axis=-1, keepdims=True)
        y = x * lax.rsqrt(ms + EPS) * gn
        partner = jnp.where(first, pltpu.roll(y, e - e // 4, 1), pltpu.roll(y, e // 4, 1))
        o_ref[c] = (y * cos + partner * sin).astype(BF16)


def _qkprep(p, gains, cos, sin, seq, tr):
    m = p.shape[1]
    nseq = seq // tr
    nblk = (GQ_HEADS + GKV_HEADS) // 2
    return pl.pallas_call(
        _qkprep_kernel,
        grid=(nblk, m // tr),
        in_specs=[
            pl.BlockSpec((2, tr, HEAD_DIM), lambda c, i: (C_QB // 2 + c, i, 0)),
            pl.BlockSpec((1, 1, HEAD_DIM), lambda c, i: (c, 0, 0)),
            pl.BlockSpec((tr, HEAD_DIM), lambda c, i: (i % nseq, 0)),
            pl.BlockSpec((tr, HEAD_DIM), lambda c, i: (i % nseq, 0)),
        ],
        out_specs=pl.BlockSpec((2, tr, HEAD_DIM), lambda c, i: (c, i, 0)),
        out_shape=jax.ShapeDtypeStruct((GQ_HEADS + GKV_HEADS, m, HEAD_DIM), BF16),
        compiler_params=_cparams(("parallel", "parallel"), 32),
        name="qkprep",
    )(p, gains, cos, sin)


def _rope_tables(seq):
    quarter = HEAD_DIM // 4
    freqs = ROPE_BASE ** (-jnp.arange(quarter, dtype=F32) / quarter)
    t = jnp.arange(seq)
    ang_r = (t // GRID_W).astype(F32)[:, None] * freqs[None, :]
    ang_c = (t % GRID_W).astype(F32)[:, None] * freqs[None, :]
    cos = jnp.concatenate([jnp.cos(ang_r), jnp.cos(ang_r), jnp.cos(ang_c), jnp.cos(ang_c)], axis=-1)
    sin = jnp.concatenate([-jnp.sin(ang_r), jnp.sin(ang_r), -jnp.sin(ang_c), jnp.sin(ang_c)], axis=-1)
    return cos, sin


def _gqa_kernel(q_ref, k_ref, v_ref, o_ref, qt_sc, st0_sc, st1_sc, m_sc, l_sc, acc_sc, *, tk):
    grp, tq, e = q_ref.shape
    seq = k_ref.shape[1]
    nq = grp * tq
    nchunks = seq // tk
    qt_sc[...] = q_ref[...].reshape(nq, e).T
    m_sc[...] = jnp.full(m_sc.shape, -jnp.inf, F32)
    l_sc[...] = jnp.zeros(l_sc.shape, F32)
    acc_sc[...] = jnp.zeros(acc_sc.shape, F32)

    def scores(i, dst):
        start = pl.multiple_of(i * tk, tk)
        dst[...] = jnp.dot(k_ref[0, pl.ds(start, tk), :], qt_sc[...], preferred_element_type=F32)

    def update(i, src):
        start = pl.multiple_of(i * tk, tk)
        vt = v_ref[0, pl.ds(start, tk), :].T
        st = src[...]
        m_old = m_sc[...]
        m_new = jnp.maximum(m_old, jnp.max(st, axis=0, keepdims=True))
        alpha = jnp.exp2(m_old - m_new)
        pt = jnp.exp2(st - m_new)
        l_sc[...] = alpha * l_sc[...] + jnp.sum(pt, axis=0, keepdims=True)
        acc_sc[...] = alpha * acc_sc[...] + jnp.dot(vt, pt.astype(BF16), preferred_element_type=F32)
        m_sc[...] = m_new

    bufs = (st0_sc, st1_sc)
    scores(0, st0_sc)

    def quad(j, carry):
        for u in range(GQA_UNROLL):
            scores(GQA_UNROLL * j + u + 1, bufs[(u + 1) % 2])
            update(GQA_UNROLL * j + u, bufs[u % 2])
        return carry

    lax.fori_loop(0, nchunks // GQA_UNROLL - 1, quad, 0)
    base = nchunks - GQA_UNROLL
    for u in range(GQA_UNROLL):
        if u + 1 < GQA_UNROLL:
            scores(base + u + 1, bufs[(u + 1) % 2])
        update(base + u, bufs[u % 2])
    o = (acc_sc[...] / l_sc[...]).T
    o_ref[...] = o.astype(BF16).reshape(grp, tq, e)


def _gqa(qk, p, bsz, seq, tq, tk):
    m = qk.shape[1]
    grp = GQ_HEADS // GKV_HEADS
    nq = seq // tq
    return pl.pallas_call(
        functools.partial(_gqa_kernel, tk=tk),
        grid=(bsz, GKV_HEADS, nq),
        in_specs=[
            pl.BlockSpec((grp, tq, HEAD_DIM), lambda b, g, i: (g, b * nq + i, 0)),
            pl.BlockSpec((1, seq, HEAD_DIM), lambda b, g, i: (GQ_HEADS + g, b, 0)),
            pl.BlockSpec((1, seq, HEAD_DIM), lambda b, g, i: (C_VB + g, b, 0)),
        ],
        out_specs=pl.BlockSpec((grp, tq, HEAD_DIM), lambda b, g, i: (g, b * nq + i, 0)),
        out_shape=jax.ShapeDtypeStruct((GQ_HEADS, m, HEAD_DIM), BF16),
        scratch_shapes=[pltpu.VMEM((HEAD_DIM, grp * tq), BF16),
                        pltpu.VMEM((tk, grp * tq), F32), pltpu.VMEM((tk, grp * tq), F32),
                        pltpu.VMEM((1, grp * tq), F32), pltpu.VMEM((1, grp * tq), F32),
                        pltpu.VMEM((HEAD_DIM, grp * tq), F32)],
        compiler_params=_cparams(("parallel", "parallel", "arbitrary"), 48),
        name="gqa",
    )(qk, qk, p)


def _na_kernel(q_ref, k_ref, v_ref, bias_ref, o_ref, sta_sc, stb_sc):
    seq = q_ref.shape[1]
    nq = NA_R * GRID_W
    nkeys = NA_KW * GRID_W
    nblk = seq // nq

    def window(b):
        lead = (NA_WIN_R // 2) * GRID_W
        if isinstance(b, int):
            if b == 0:
                return 0, 0, 0
            if b == nblk - 1:
                return seq - nq, seq - nkeys, 2
            return b * nq, b * nq - lead, 1
        q0 = pl.multiple_of(b * nq, nq)
        return q0, pl.multiple_of(q0 - lead, GRID_W), 1

    def scores(b, dst):
        q0, k0, variant = window(b)
        qt = (q_ref[0, pl.ds(q0, nq), :].astype(F32) * QSCALE).astype(BF16).T
        k = k_ref[0, pl.ds(k0, nkeys), :]
        dst[...] = jnp.dot(k, qt, preferred_element_type=F32) + bias_ref[variant, 0]

    def update(b, src):
        q0, k0, _ = window(b)
        st = src[...]
        vt = v_ref[0, pl.ds(k0, nkeys), :].T
        pt = jnp.exp2(st - jnp.max(st, axis=0, keepdims=True))
        l = jnp.sum(pt, axis=0, keepdims=True)
        ot = jnp.dot(vt, pt.astype(BF16), preferred_element_type=F32)
        o_ref[0, pl.ds(q0, nq), :] = (ot / l).T.astype(BF16)

    scores(0, sta_sc)
    scores(1, stb_sc)
    update(0, sta_sc)

    def pair(j, carry):
        scores(2 * j, sta_sc)
        update(2 * j - 1, stb_sc)
        scores(2 * j + 1, stb_sc)
        update(2 * j, sta_sc)
        return carry

    lax.fori_loop(1, nblk // 2 - 1, pair, 0)
    scores(nblk - 2, sta_sc)
    update(nblk - 3, stb_sc)
    scores(nblk - 1, stb_sc)
    update(nblk - 2, sta_sc)
    update(nblk - 1, stb_sc)


def _na(p, bias, layer, bsz, seq):
    m = p.shape[1]
    nq = NA_R * GRID_W
    nkeys = NA_KW * GRID_W
    return pl.pallas_call(
        _na_kernel,
        grid=(NA_HEADS, bsz),
        in_specs=[
            pl.BlockSpec((1, seq, HEAD_DIM), lambda h, b: (C_QA + h, b, 0)),
            pl.BlockSpec((1, seq, HEAD_DIM), lambda h, b: (C_KA + h, b, 0)),
            pl.BlockSpec((1, seq, HEAD_DIM), lambda h, b: (C_VA + h, b, 0)),
            pl.BlockSpec((None, 3, 1, nkeys, nq), lambda h, b: (layer, 0, h, 0, 0)),
        ],
        out_specs=pl.BlockSpec((1, seq, HEAD_DIM), lambda h, b: (h, b, 0)),
        out_shape=jax.ShapeDtypeStruct((NA_HEADS, m, HEAD_DIM), BF16),
        scratch_shapes=[pltpu.VMEM((nkeys, nq), F32), pltpu.VMEM((nkeys, nq), F32)],
        compiler_params=_cparams(("parallel", "parallel"), 48),
        name="na",
    )(p, p, p, bias)


def _na_bias(rpb, seq):
    rows = seq // GRID_W
    kr = min(NA_WIN_R, rows)
    ndr = 2 * NA_WIN_R - 1
    ndc = 2 * NA_WIN_C - 1
    j = np.arange(GRID_W)[:, None]
    c = np.arange(GRID_W)[None, :]
    cs = np.clip(j - NA_WIN_C // 2, 0, GRID_W - NA_WIN_C)
    col_ok = (c >= cs) & (c < cs + NA_WIN_C)
    dc = np.clip(c - j + NA_WIN_C - 1, 0, ndc - 1)
    onehot = (dc[:, :, None] == np.arange(ndc)[None, None, :]).astype(np.float32)
    tiles = jnp.einsum('lhrd,jcd->lhrjc', rpb.astype(F32), jnp.asarray(onehot),
                       precision=lax.Precision.HIGHEST)
    tiles = jnp.where(jnp.asarray(col_ok), tiles * LOG2E, NEG_INF)
    masked = jnp.full(tiles.shape[:2] + (1, GRID_W, GRID_W), NEG_INF, F32)
    tiles = jnp.concatenate([tiles, masked], axis=2)
    variants = []
    for r0, k0 in ((0, 0), (NA_R, NA_R - kr // 2), (rows - NA_R, rows - NA_KW)):
        r = r0 + np.arange(NA_R)[:, None]
        krow = k0 + np.arange(NA_KW)[None, :]
        start = np.clip(r - kr // 2, 0, rows - kr)
        row_ok = (krow >= start) & (krow < start + kr)
        dr = np.where(row_ok, krow - r + NA_WIN_R - 1, ndr)
        sel = jnp.concatenate([tiles[:, :, d:d + 1] for d in dr.T.reshape(-1)], axis=2)
        sel = sel.reshape(sel.shape[:2] + (NA_KW, NA_R, GRID_W, GRID_W))
        sel = sel.transpose(0, 1, 2, 5, 3, 4)
        variants.append(sel.reshape(sel.shape[:2] + (NA_KW * GRID_W, NA_R * GRID_W)))
    return jnp.stack(variants, axis=1)


def _dil_windows(seq):
    out = []
    for win, _ in DIL_GROUPS:
        hw = -(-(win // 2) // DIL_T) * DIL_T
        out.append((hw, min(2 * hw + DIL_T, seq)))
    return out


def _dil_kernel(q0_ref, q1_ref, q2_ref, k0_ref, k1_ref, k2_ref, v0_ref, v1_ref, v2_ref,
                e0_ref, e1_ref, e2_ref, o_ref, sta_sc, stb_sc):
    q_refs = (q0_ref, q1_ref, q2_ref)
    k_refs = (k0_ref, k1_ref, k2_ref)
    v_refs = (v0_ref, v1_ref, v2_ref)
    e_refs = (e0_ref, e1_ref, e2_ref)
    seq = k0_ref.shape[1]
    nsub = q0_ref.shape[1] // DIL_T
    windows = _dil_windows(seq)
    offs = [sum(w for _, w in windows[:g]) for g in range(len(windows))]

    def key_start(sub, g):
        t0 = pl.program_id(2) * q0_ref.shape[1] + sub * DIL_T
        hw, wlen = windows[g]
        ws = pl.multiple_of(jnp.clip(t0 - hw, 0, seq - wlen), DIL_T)
        return ws, pl.multiple_of(wlen - DIL_T - (t0 - ws), DIL_T)

    def scores(sub, dst):
        for g, (hw, wlen) in enumerate(windows):
            ws, erow = key_start(sub, g)
            qt = (q_refs[g][0, pl.ds(sub * DIL_T, DIL_T), :].astype(F32) * QSCALE).astype(BF16).T
            k = k_refs[g][0, pl.ds(ws, wlen), :]
            dst[pl.ds(offs[g], wlen), :] = (jnp.dot(k, qt, preferred_element_type=F32)
                                            + e_refs[g][0, pl.ds(erow, wlen), :])

    def update(sub, src):
        st = src[...]
        pt = jnp.exp2(st - jnp.max(st, axis=0, keepdims=True))
        l = jnp.sum(pt, axis=0, keepdims=True)
        pt = pt.astype(BF16)
        acc = jnp.zeros((HEAD_DIM, DIL_T), F32)
        for g, (hw, wlen) in enumerate(windows):
            ws, _ = key_start(sub, g)
            vt = v_refs[g][0, pl.ds(ws, wlen), :].T
            acc = acc + jnp.dot(vt, pt[offs[g]:offs[g] + wlen], preferred_element_type=F32)
        o_ref[0, pl.ds(sub * DIL_T, DIL_T), :] = (acc / l).T.astype(BF16)

    bufs = (sta_sc, stb_sc)
    scores(0, bufs[0])
    for sub in range(nsub):
        if sub + 1 < nsub:
            scores(sub + 1, bufs[(sub + 1) % 2])
        update(sub, bufs[sub % 2])


def _dil(p, tables, bsz, seq):
    m = p.shape[1]
    tq = min(DIL_Q, seq)
    nq = seq // tq
    hg = DIL_HEADS_PER_GROUP
    nkeys = sum(w for _, w in _dil_windows(seq))

    def qspec(g):
        return pl.BlockSpec((1, tq, HEAD_DIM), lambda b, h, i: (C_QC + hg * g + h, b * nq + i, 0))

    def kvspec(base, g):
        return _resident((1, seq, HEAD_DIM), lambda b, h, i: (base + hg * g + h, b, 0))

    def tspec(t):
        return _resident((1,) + t.shape[1:], lambda b, h, i: (h, 0, 0))

    return pl.pallas_call(
        _dil_kernel,
        grid=(bsz, hg, nq),
        in_specs=[qspec(0), qspec(1), qspec(2),
                  kvspec(C_KC, 0), kvspec(C_KC, 1), kvspec(C_KC, 2),
                  kvspec(C_VC, 0), kvspec(C_VC, 1), kvspec(C_VC, 2),
                  tspec(tables[0]), tspec(tables[1]), tspec(tables[2])],
        out_specs=pl.BlockSpec((1, tq, HEAD_DIM), lambda b, h, i: (h, b * nq + i, 0)),
        out_shape=jax.ShapeDtypeStruct((hg, m, HEAD_DIM), BF16),
        scratch_shapes=[pltpu.VMEM((nkeys, DIL_T), F32), pltpu.VMEM((nkeys, DIL_T), F32)],
        compiler_params=_cparams(("parallel", "parallel", "arbitrary"), 56),
        name="dilated",
    )(p, p, p, p, p, p, p, p, p, *tables)


def _dil_tables(seq):
    slopes = 2.0 ** (-ALIBI_MAX_EXP * jnp.arange(1, DIL_HEADS + 1, dtype=F32) / DIL_HEADS)
    out = []
    for g, (hw, wlen) in enumerate(_dil_windows(seq)):
        win, dil = DIL_GROUPS[g]
        r = np.arange(2 * wlen - DIL_T)[:, None]
        i = np.arange(DIL_T)[None, :]
        rel = i - r + wlen - DIL_T
        ok = jnp.asarray((np.abs(rel) <= win // 2) & (rel % dil == 0))
        dist = jnp.asarray(np.abs(rel), F32)
        sl = slopes[DIL_HEADS_PER_GROUP * g:DIL_HEADS_PER_GROUP * (g + 1)] * LOG2E
        out.append(jnp.where(ok[None], -sl[:, None, None] * dist[None], NEG_INF))
    return out


def _merge_kernel(ya_ref, yb_ref, yc_ref, za0_ref, za1_ref, zb0_ref, zb1_ref, zc_ref, ga_ref, gb_ref, gc_ref,
                  bg_ref, x_ref, wa_ref, wb_ref, wc_ref, wo_ref, pg_ref, o_ref):
    d = x_ref.shape[1]

    def cat(*refs):
        return jnp.concatenate([r[c] for r in refs for c in range(r.shape[0])], axis=-1).astype(F32)

    def branch(y_ref, z_refs, w_ref):
        z = cat(*z_refs)
        u = (cat(y_ref) * (z * jax.nn.sigmoid(z))).astype(BF16)
        return jnp.dot(u, w_ref[...], preferred_element_type=F32)

    def gate(g_ref, k):
        return jax.nn.sigmoid(cat(g_ref) + bg_ref[:, k * d:(k + 1) * d])

    merged = (gate(ga_ref, 0) * branch(ya_ref, (za0_ref, za1_ref), wa_ref)
              + gate(gb_ref, 1) * branch(yb_ref, (zb0_ref, zb1_ref), wb_ref)
              + gate(gc_ref, 2) * branch(yc_ref, (zc_ref,), wc_ref))
    out = jnp.dot(merged.astype(BF16), wo_ref[...], preferred_element_type=F32)
    ms = jnp.mean(out * out, axis=-1, keepdims=True)
    o_ref[...] = x_ref[...] + out * lax.rsqrt(ms + EPS) * pg_ref[...]


def _merge(ya, yb, yc, p, bg, x2d, wa, wb, wc, wo, pg, layer, tm):
    m, d = x2d.shape
    gch = d // V7X_LANES

    zh = NA_HEADS // 2

    def chunks(n, first):
        assert first % n == 0
        return pl.BlockSpec((n, tm, HEAD_DIM), lambda i: (first // n, i, 0))

    def const(shape):
        return _resident(shape, lambda i: (0, 0))

    def weight(w):
        return _resident((None,) + w.shape[1:], lambda i: (layer, 0, 0))

    return pl.pallas_call(
        _merge_kernel,
        grid=(m // tm,),
        in_specs=[
            chunks(NA_HEADS, 0), chunks(GQ_HEADS, 0), chunks(DIL_HEADS_PER_GROUP, 0),
            chunks(zh, C_ZA), chunks(zh, C_ZA + zh), chunks(zh, C_ZB), chunks(zh, C_ZB + zh),
            chunks(DIL_HEADS_PER_GROUP, C_ZC),
            chunks(gch, C_GATE), chunks(gch, C_GATE + gch), chunks(gch, C_GATE + 2 * gch),
            const((1, 3 * d)),
            pl.BlockSpec((tm, d), lambda i: (i, 0)),
            weight(wa), weight(wb), weight(wc), weight(wo),
            const((1, d)),
        ],
        out_specs=pl.BlockSpec((tm, d), lambda i: (i, 0)),
        out_shape=jax.ShapeDtypeStruct((m, d), F32),
        compiler_params=_cparams(("parallel",), 56),
        name="merge",
    )(ya, yb, yc, p, p, p, p, p, p, p, p, bg.reshape(1, 3 * d), x2d, wa, wb, wc, wo, pg.reshape(1, d))


def _layer(x2d, layer, bsz, seq, pre_g, w_in, b_gate, q_g, k_g, na_bias, wa, wb, wc, wo, post_g, cos, sin,
           tables):
    p = _inproj(x2d, pre_g, w_in, layer, tm=min(1024, x2d.shape[0]))
    nprep = (GQ_HEADS + GKV_HEADS) // 2
    gains = jnp.concatenate([jnp.broadcast_to(q_g * QSCALE, (GQ_HEADS // 2, HEAD_DIM)),
                             jnp.broadcast_to(k_g, (GKV_HEADS // 2, HEAD_DIM))], axis=0)
    qk = _qkprep(p, gains.reshape(nprep, 1, HEAD_DIM), cos, sin, seq, tr=min(1024, seq))
    ya = _na(p, na_bias, layer, bsz, seq)
    yb = _gqa(qk, p, bsz, seq, tq=min(256, seq), tk=min(512, seq // GQA_UNROLL))
    yc = _dil(p, tables, bsz, seq)
    return _merge(ya, yb, yc, p, b_gate, x2d, wa, wb, wc, wo, post_g, layer, tm=256)


def kernel(x, pre_norm_g, w_in, b_gate, q_norm_g, k_norm_g, rpb, w_branch_a, w_branch_b, w_branch_c, w_out,
           post_norm_g):
    bsz, seq, d = x.shape
    depth = w_in.shape[0]
    cos, sin = _rope_tables(seq)
    tables = _dil_tables(seq)
    na_bias = _na_bias(rpb, seq)
    w_in, wa, wb, wc, wo = (w.astype(BF16) for w in (w_in, w_branch_a, w_branch_b, w_branch_c, w_out))
    x2d = x.reshape(bsz * seq, d)
    for l in range(depth):
        x2d = _layer(x2d, l, bsz, seq, pre_norm_g[l], w_in, b_gate[l], q_norm_g[l], k_norm_g[l], na_bias,
                     wa, wb, wc, wo, post_norm_g[l], cos, sin, tables)
    return x2d.reshape(bsz, seq, d)
```

```python
import functools
import math

import numpy as np
import jax
import jax.numpy as jnp
from jax import lax
from jax.experimental import pallas as pl
from jax.experimental.pallas import tpu as pltpu

F32 = jnp.float32
BF16 = jnp.bfloat16

HEAD_DIM = 128
GRID_W = 64
EPS = 1e-6
NEG_INF = -1e30
NA_HEADS = 8
NA_WIN_R = 8
NA_WIN_C = 16
GQ_HEADS = 8
GKV_HEADS = 2
ROPE_BASE = 10000.0
DIL_GROUPS = ((128, 1), (512, 4), (2048, 16))
DIL_HEADS_PER_GROUP = 4
DIL_HEADS = 12
ALIBI_MAX_EXP = 8.0
LOG2E = math.log2(math.e)
QSCALE = HEAD_DIM ** -0.5 * LOG2E

V7X_LANES = 128
MIB = 1024 * 1024

CHUNKS_PER_TILE = 20
SUB_CHUNKS = 4
N_CHUNKS = 140
N_TILES = N_CHUNKS // CHUNKS_PER_TILE
TILE_ROT = 3
C_VC = 0
C_ZA, C_ZB, C_ZC = 12, 20, 28
C_GATE = 32
C_QA, C_KA, C_VA = 80, 88, 96
C_QB, C_KB, C_VB = 104, 112, 114
C_QC, C_KC = 116, 128

NA_R = 4
NA_KW = NA_R + NA_WIN_R - 1
GQA_UNROLL = 4
GQA_TILES = 4
DIL_T = 256
DIL_Q = 4 * DIL_T
DIL_ROWS = 256
DIL_ALIGN = 64


def _cparams(sem, vmem_mib):
    return pltpu.CompilerParams(dimension_semantics=sem, vmem_limit_bytes=int(vmem_mib * MIB))


def _resident(shape, index_map):
    return pl.BlockSpec(shape, index_map, pipeline_mode=pl.Buffered(1))


def _inproj_kernel(x_ref, g_ref, w_ref, p_ref, xn_ref):
    @pl.when(pl.program_id(1) == 0)
    def _():
        x = x_ref[...]
        ms = jnp.mean(x * x, axis=-1, keepdims=True)
        xn_ref[...] = (x * lax.rsqrt(ms + EPS) * g_ref[...]).astype(BF16)

    sub_w = SUB_CHUNKS * V7X_LANES
    for s in range(CHUNKS_PER_TILE // SUB_CHUNKS):
        res = jnp.dot(xn_ref[...], w_ref[:, s * sub_w:(s + 1) * sub_w], preferred_element_type=F32)
        for c in range(SUB_CHUNKS):
            p_ref[s * SUB_CHUNKS + c] = res[:, c * V7X_LANES:(c + 1) * V7X_LANES].astype(BF16)


def _inproj(x2d, g, w, layer, tm):
    m, d = x2d.shape
    tn = CHUNKS_PER_TILE * V7X_LANES
    return pl.pallas_call(
        _inproj_kernel,
        grid=(m // tm, N_TILES),
        in_specs=[
            pl.BlockSpec((tm, d), lambda i, j: (i, 0)),
            pl.BlockSpec((1, d), lambda i, j: (0, 0)),
            pl.BlockSpec((None, d, tn), lambda i, j: (layer, 0, (j + TILE_ROT) % N_TILES)),
        ],
        out_specs=pl.BlockSpec((CHUNKS_PER_TILE, tm, V7X_LANES), lambda i, j: (j, i, 0)),
        out_shape=jax.ShapeDtypeStruct((N_CHUNKS, m, V7X_LANES), BF16),
        scratch_shapes=[pltpu.VMEM((tm, d), BF16)],
        compiler_params=_cparams(("parallel", "arbitrary"), 58),
        name="inproj",
    )(x2d, g.reshape(1, d), w)


def _qkprep_kernel(p_ref, gn_ref, cos_ref, sin_ref, o_ref):
    nc, tr, e = p_ref.shape
    lane = lax.broadcasted_iota(jnp.int32, (tr, e), 1)
    first = (lane % (e // 2)) < (e // 4)
    cos = cos_ref[...]
    sin = sin_ref[...]
    gn = gn_ref[0]
    for c in range(nc):
        x = p_ref[c].astype(F32)
        ms = jnp.mean(x * x, axis=-1, keepdims=True)
        y = x * lax.rsqrt(ms + EPS) * gn
        partner = jnp.where(first, pltpu.roll(y, e - e // 4, 1), pltpu.roll(y, e // 4, 1))
        o_ref[c] = (y * cos + partner * sin).astype(BF16)


def _qkprep(p, gains, cos, sin, seq, tr):
    m = p.shape[1]
    nseq = seq // tr
    nblk = (GQ_HEADS + GKV_HEADS) // 2
    return pl.pallas_call(
        _qkprep_kernel,
        grid=(nblk, m // tr),
        in_specs=[
            pl.BlockSpec((2, tr, HEAD_DIM), lambda c, i: (C_QB // 2 + c, i, 0)),
            pl.BlockSpec((1, 1, HEAD_DIM), lambda c, i: (c, 0, 0)),
            pl.BlockSpec((tr, HEAD_DIM), lambda c, i: (i % nseq, 0)),
            pl.BlockSpec((tr, HEAD_DIM), lambda c, i: (i % nseq, 0)),
        ],
        out_specs=pl.BlockSpec((2, tr, HEAD_DIM), lambda c, i: (c, i, 0)),
        out_shape=jax.ShapeDtypeStruct((GQ_HEADS + GKV_HEADS, m, HEAD_DIM), BF16),
        compiler_params=_cparams(("parallel", "parallel"), 32),
        name="qkprep",
    )(p, gains, cos, sin)


def _rope_tables(seq):
    quarter = HEAD_DIM // 4
    freqs = ROPE_BASE ** (-jnp.arange(quarter, dtype=F32) / quarter)
    t = jnp.arange(seq)
    ang_r = (t // GRID_W).astype(F32)[:, None] * freqs[None, :]
    ang_c = (t % GRID_W).astype(F32)[:, None] * freqs[None, :]
    cos = jnp.concatenate([jnp.cos(ang_r), jnp.cos(ang_r), jnp.cos(ang_c), jnp.cos(ang_c)], axis=-1)
    sin = jnp.concatenate([-jnp.sin(ang_r), jnp.sin(ang_r), -jnp.sin(ang_c), jnp.sin(ang_c)], axis=-1)
    return cos, sin


def _gqa_kernel(q_ref, k_ref, v_ref, o_ref, qt_sc, st0_sc, st1_sc, m_sc, l_sc, acc_sc, *, tq, tk):
    grp, tq_step, e = q_ref.shape
    seq = k_ref.shape[1]
    ntiles = tq_step // tq
    nq = grp * tq
    nchunks = seq // tk
    for t in range(ntiles):
        qt_sc[t] = q_ref[:, pl.ds(t * tq, tq), :].reshape(nq, e).T

    def scores(t, i, dst):
        start = pl.multiple_of(i * tk, tk)
        dst[...] = jnp.dot(k_ref[0, pl.ds(start, tk), :], qt_sc[t], preferred_element_type=F32)

    def update(i, src):
        start = pl.multiple_of(i * tk, tk)
        vt = v_ref[0, pl.ds(start, tk), :].T
        st = src[...]
        m_old = m_sc[...]
        m_new = jnp.maximum(m_old, jnp.max(st, axis=0, keepdims=True))
        alpha = jnp.exp2(m_old - m_new)
        pt = jnp.exp2(st - m_new)
        l_sc[...] = alpha * l_sc[...] + jnp.sum(pt, axis=0, keepdims=True)
        acc_sc[...] = alpha * acc_sc[...] + jnp.dot(vt, pt.astype(BF16), preferred_element_type=F32)
        m_sc[...] = m_new

    bufs = (st0_sc, st1_sc)
    scores(0, 0, st0_sc)
    for t in range(ntiles):
        m_sc[...] = jnp.full(m_sc.shape, -jnp.inf, F32)
        l_sc[...] = jnp.zeros(l_sc.shape, F32)
        acc_sc[...] = jnp.zeros(acc_sc.shape, F32)

        def quad(j, carry, t=t):
            for u in range(GQA_UNROLL):
                scores(t, GQA_UNROLL * j + u + 1, bufs[(u + 1) % 2])
                update(GQA_UNROLL * j + u, bufs[u % 2])
            return carry

        lax.fori_loop(0, nchunks // GQA_UNROLL - 1, quad, 0)
        base = nchunks - GQA_UNROLL
        for u in range(GQA_UNROLL):
            if u + 1 < GQA_UNROLL:
                scores(t, base + u + 1, bufs[(u + 1) % 2])
            elif t + 1 < ntiles:
                scores(t + 1, 0, bufs[(u + 1) % 2])
            update(base + u, bufs[u % 2])
        o = (acc_sc[...] / l_sc[...]).T
        o_ref[:, pl.ds(t * tq, tq), :] = o.astype(BF16).reshape(grp, tq, e)


def _gqa(qk, p, bsz, seq, tq, tk, ntiles):
    m = qk.shape[1]
    grp = GQ_HEADS // GKV_HEADS
    tq_step = tq * ntiles
    nq = seq // tq_step
    return pl.pallas_call(
        functools.partial(_gqa_kernel, tq=tq, tk=tk),
        grid=(bsz, GKV_HEADS, nq),
        in_specs=[
            pl.BlockSpec((grp, tq_step, HEAD_DIM), lambda b, g, i: (g, b * nq + i, 0)),
            pl.BlockSpec((1, seq, HEAD_DIM), lambda b, g, i: (GQ_HEADS + g, b, 0)),
            pl.BlockSpec((1, seq, HEAD_DIM), lambda b, g, i: (C_VB + g, b, 0)),
        ],
        out_specs=pl.BlockSpec((grp, tq_step, HEAD_DIM), lambda b, g, i: (g, b * nq + i, 0)),
        out_shape=jax.ShapeDtypeStruct((GQ_HEADS, m, HEAD_DIM), BF16),
        scratch_shapes=[pltpu.VMEM((ntiles, HEAD_DIM, grp * tq), BF16),
                        pltpu.VMEM((tk, grp * tq), F32), pltpu.VMEM((tk, grp * tq), F32),
                        pltpu.VMEM((1, grp * tq), F32), pltpu.VMEM((1, grp * tq), F32),
                        pltpu.VMEM((HEAD_DIM, grp * tq), F32)],
        compiler_params=_cparams(("parallel", "parallel", "arbitrary"), 48),
        name="gqa",
    )(qk, qk, p)


def _na_kernel(q_ref, k_ref, v_ref, bias_ref, o_ref, sta_sc, stb_sc):
    seq = q_ref.shape[1]
    nq = NA_R * GRID_W
    nkeys = NA_KW * GRID_W
    nblk = seq // nq

    def window(b):
        lead = (NA_WIN_R // 2) * GRID_W
        if isinstance(b, int):
            if b == 0:
                return 0, 0, 0
            if b == nblk - 1:
                return seq - nq, seq - nkeys, 2
            return b * nq, b * nq - lead, 1
        q0 = pl.multiple_of(b * nq, nq)
        return q0, pl.multiple_of(q0 - lead, GRID_W), 1

    def scores(b, dst):
        q0, k0, variant = window(b)
        qt = (q_ref[0, pl.ds(q0, nq), :].astype(F32) * QSCALE).astype(BF16).T
        k = k_ref[0, pl.ds(k0, nkeys), :]
        dst[...] = jnp.dot(k, qt, preferred_element_type=F32) + bias_ref[variant, 0]

    def update(b, src):
        q0, k0, _ = window(b)
        st = src[...]
        vt = v_ref[0, pl.ds(k0, nkeys), :].T
        pt = jnp.exp2(st - jnp.max(st, axis=0, keepdims=True))
        l = jnp.sum(pt, axis=0, keepdims=True)
        ot = jnp.dot(vt, pt.astype(BF16), preferred_element_type=F32)
        o_ref[0, pl.ds(q0, nq), :] = (ot / l).T.astype(BF16)

    scores(0, sta_sc)
    scores(1, stb_sc)
    update(0, sta_sc)

    def pair(j, carry):
        scores(2 * j, sta_sc)
        update(2 * j - 1, stb_sc)
        scores(2 * j + 1, stb_sc)
        update(2 * j, sta_sc)
        return carry

    lax.fori_loop(1, nblk // 2 - 1, pair, 0)
    scores(nblk - 2, sta_sc)
    update(nblk - 3, stb_sc)
    scores(nblk - 1, stb_sc)
    update(nblk - 2, sta_sc)
    update(nblk - 1, stb_sc)


def _na(p, bias, layer, bsz, seq):
    m = p.shape[1]
    nq = NA_R * GRID_W
    nkeys = NA_KW * GRID_W
    return pl.pallas_call(
        _na_kernel,
        grid=(NA_HEADS, bsz),
        in_specs=[
            pl.BlockSpec((1, seq, HEAD_DIM), lambda h, b: (C_QA + h, b, 0)),
            pl.BlockSpec((1, seq, HEAD_DIM), lambda h, b: (C_KA + h, b, 0)),
            pl.BlockSpec((1, seq, HEAD_DIM), lambda h, b: (C_VA + h, b, 0)),
            pl.BlockSpec((None, 3, 1, nkeys, nq), lambda h, b: (layer, 0, h, 0, 0)),
        ],
        out_specs=pl.BlockSpec((1, seq, HEAD_DIM), lambda h, b: (h, b, 0)),
        out_shape=jax.ShapeDtypeStruct((NA_HEADS, m, HEAD_DIM), BF16),
        scratch_shapes=[pltpu.VMEM((nkeys, nq), F32), pltpu.VMEM((nkeys, nq), F32)],
        compiler_params=_cparams(("parallel", "parallel"), 48),
        name="na",
    )(p, p, p, bias)


def _na_bias(rpb, seq):
    rows = seq // GRID_W
    kr = min(NA_WIN_R, rows)
    ndr = 2 * NA_WIN_R - 1
    ndc = 2 * NA_WIN_C - 1
    j = np.arange(GRID_W)[:, None]
    c = np.arange(GRID_W)[None, :]
    cs = np.clip(j - NA_WIN_C // 2, 0, GRID_W - NA_WIN_C)
    col_ok = (c >= cs) & (c < cs + NA_WIN_C)
    dc = np.clip(c - j + NA_WIN_C - 1, 0, ndc - 1)
    onehot = (dc[:, :, None] == np.arange(ndc)[None, None, :]).astype(np.float32)
    tiles = jnp.einsum('lhrd,jcd->lhrjc', rpb.astype(F32), jnp.asarray(onehot),
                       precision=lax.Precision.HIGHEST)
    tiles = jnp.where(jnp.asarray(col_ok), tiles * LOG2E, NEG_INF)
    masked = jnp.full(tiles.shape[:2] + (1, GRID_W, GRID_W), NEG_INF, F32)
    tiles = jnp.concatenate([tiles, masked], axis=2)
    variants = []
    for r0, k0 in ((0, 0), (NA_R, NA_R - kr // 2), (rows - NA_R, rows - NA_KW)):
        r = r0 + np.arange(NA_R)[:, None]
        krow = k0 + np.arange(NA_KW)[None, :]
        start = np.clip(r - kr // 2, 0, rows - kr)
        row_ok = (krow >= start) & (krow < start + kr)
        dr = np.where(row_ok, krow - r + NA_WIN_R - 1, ndr)
        sel = jnp.concatenate([tiles[:, :, d:d + 1] for d in dr.T.reshape(-1)], axis=2)
        sel = sel.reshape(sel.shape[:2] + (NA_KW, NA_R, GRID_W, GRID_W))
        sel = sel.transpose(0, 1, 2, 5, 3, 4)
        variants.append(sel.reshape(sel.shape[:2] + (NA_KW * GRID_W, NA_R * GRID_W)))
    return jnp.stack(variants, axis=1)


def _dil_windows(seq):
    out = []
    for win, _ in DIL_GROUPS:
        hw = -(-(win // 2) // DIL_ALIGN) * DIL_ALIGN
        out.append((hw, min(2 * hw + DIL_T, seq)))
    return out


def _dil_kernel(q0_ref, q1_ref, q2_ref, k0_ref, k1_ref, k2_ref, v0_ref, v1_ref, v2_ref,
                e0_ref, e1_ref, e2_ref, o_ref, sta_sc, stb_sc, mxa_sc, mxb_sc, pt_sc):
    q_refs = (q0_ref, q1_ref, q2_ref)
    k_refs = (k0_ref, k1_ref, k2_ref)
    v_refs = (v0_ref, v1_ref, v2_ref)
    e_refs = (e0_ref, e1_ref, e2_ref)
    seq = k0_ref.shape[1]
    nsub = q0_ref.shape[1] // DIL_T
    windows = _dil_windows(seq)
    offs = [sum(w for _, w in windows[:g]) for g in range(len(windows))]

    def key_start(sub, g):
        t0 = pl.program_id(2) * q0_ref.shape[1] + sub * DIL_T
        hw, wlen = windows[g]
        ws = pl.multiple_of(jnp.clip(t0 - hw, 0, seq - wlen), DIL_ALIGN)
        return ws, pl.multiple_of(wlen - DIL_T - (t0 - ws), DIL_ALIGN)

    def scores(sub, dst, mx_dst):
        m8 = None
        for g, (hw, wlen) in enumerate(windows):
            ws, erow = key_start(sub, g)
            qt = (q_refs[g][0, pl.ds(sub * DIL_T, DIL_T), :].astype(F32) * QSCALE).astype(BF16).T
            for r in range(0, wlen, DIL_ROWS):
                n = min(DIL_ROWS, wlen - r)
                k = k_refs[g][0, pl.ds(ws + r, n), :]
                st = (jnp.dot(k, qt, preferred_element_type=F32)
                      + e_refs[g][0, pl.ds(erow + r, n), :])
                dst[pl.ds(offs[g] + r, n), :] = st
                m = jnp.max(st.reshape(n // 8, 8, DIL_T), axis=0)
                m8 = m if m8 is None else jnp.maximum(m8, m)
        mx_dst[...] = m8

    def update(sub, src, mx_src):
        mx = jnp.max(mx_src[...], axis=0, keepdims=True)
        nkeys = src.shape[0]
        l8 = jnp.zeros((8, DIL_T), F32)
        for r in range(0, nkeys, DIL_ROWS):
            n = min(DIL_ROWS, nkeys - r)
            pt = jnp.exp2(src[pl.ds(r, n), :] - mx)
            l8 = l8 + jnp.sum(pt.reshape(n // 8, 8, DIL_T), axis=0)
            pt_sc[pl.ds(r, n), :] = pt.astype(BF16)
        l = jnp.sum(l8, axis=0, keepdims=True)
        acc = jnp.zeros((HEAD_DIM, DIL_T), F32)
        for g, (hw, wlen) in enumerate(windows):
            ws, _ = key_start(sub, g)
            vt = v_refs[g][0, pl.ds(ws, wlen), :].T
            acc = acc + jnp.dot(vt, pt_sc[pl.ds(offs[g], wlen), :], preferred_element_type=F32)
        o_ref[0, pl.ds(sub * DIL_T, DIL_T), :] = (acc / l).T.astype(BF16)

    bufs = ((sta_sc, mxa_sc), (stb_sc, mxb_sc))
    scores(0, *bufs[0])
    for sub in range(nsub):
        if sub + 1 < nsub:
            scores(sub + 1, *bufs[(sub + 1) % 2])
        update(sub, *bufs[sub % 2])


def _dil(p, tables, bsz, seq):
    m = p.shape[1]
    tq = min(DIL_Q, seq)
    nq = seq // tq
    hg = DIL_HEADS_PER_GROUP
    nkeys = sum(w for _, w in _dil_windows(seq))

    def qspec(g):
        return pl.BlockSpec((1, tq, HEAD_DIM), lambda b, h, i: (C_QC + hg * g + h, b * nq + i, 0))

    def kvspec(base, g):
        return _resident((1, seq, HEAD_DIM), lambda b, h, i: (base + hg * g + h, b, 0))

    def tspec(t):
        return _resident((1,) + t.shape[1:], lambda b, h, i: (h, 0, 0))

    return pl.pallas_call(
        _dil_kernel,
        grid=(bsz, hg, nq),
        in_specs=[qspec(0), qspec(1), qspec(2),
                  kvspec(C_KC, 0), kvspec(C_KC, 1), kvspec(C_KC, 2),
                  kvspec(C_VC, 0), kvspec(C_VC, 1), kvspec(C_VC, 2),
                  tspec(tables[0]), tspec(tables[1]), tspec(tables[2])],
        out_specs=pl.BlockSpec((1, tq, HEAD_DIM), lambda b, h, i: (h, b * nq + i, 0)),
        out_shape=jax.ShapeDtypeStruct((hg, m, HEAD_DIM), BF16),
        scratch_shapes=[pltpu.VMEM((nkeys, DIL_T), F32), pltpu.VMEM((nkeys, DIL_T), F32),
                        pltpu.VMEM((8, DIL_T), F32), pltpu.VMEM((8, DIL_T), F32),
                        pltpu.VMEM((nkeys, DIL_T), BF16)],
        compiler_params=_cparams(("parallel", "parallel", "arbitrary"), 56),
        name="dilated",
    )(p, p, p, p, p, p, p, p, p, *tables)


def _dil_tables(seq):
    slopes = 2.0 ** (-ALIBI_MAX_EXP * jnp.arange(1, DIL_HEADS + 1, dtype=F32) / DIL_HEADS)
    out = []
    for g, (hw, wlen) in enumerate(_dil_windows(seq)):
        win, dil = DIL_GROUPS[g]
        r = np.arange(2 * wlen - DIL_T)[:, None]
        i = np.arange(DIL_T)[None, :]
        rel = i - r + wlen - DIL_T
        ok = jnp.asarray((np.abs(rel) <= win // 2) & (rel % dil == 0))
        dist = jnp.asarray(np.abs(rel), F32)
        sl = slopes[DIL_HEADS_PER_GROUP * g:DIL_HEADS_PER_GROUP * (g + 1)] * LOG2E
        out.append(jnp.where(ok[None], -sl[:, None, None] * dist[None], NEG_INF))
    return out


def _merge_kernel(ya_ref, yb_ref, yc_ref, za0_ref, za1_ref, zb0_ref, zb1_ref, zc_ref, ga_ref, gb_ref, gc_ref,
                  bg_ref, x_ref, wa_ref, wb_ref, wc_ref, wo_ref, pg_ref, o_ref):
    d = x_ref.shape[1]

    def cat(*refs):
        return jnp.concatenate([r[c] for r in refs for c in range(r.shape[0])], axis=-1).astype(F32)

    def branch(y_ref, z_refs, w_ref):
        z = cat(*z_refs)
        u = (cat(y_ref) * (z * jax.nn.sigmoid(z))).astype(BF16)
        return jnp.dot(u, w_ref[...], preferred_element_type=F32)

    def gate(g_ref, k):
        return jax.nn.sigmoid(cat(g_ref) + bg_ref[:, k * d:(k + 1) * d])

    merged = (gate(ga_ref, 0) * branch(ya_ref, (za0_ref, za1_ref), wa_ref)
              + gate(gb_ref, 1) * branch(yb_ref, (zb0_ref, zb1_ref), wb_ref)
              + gate(gc_ref, 2) * branch(yc_ref, (zc_ref,), wc_ref))
    out = jnp.dot(merged.astype(BF16), wo_ref[...], preferred_element_type=F32)
    ms = jnp.mean(out * out, axis=-1, keepdims=True)
    o_ref[...] = x_ref[...] + out * lax.rsqrt(ms + EPS) * pg_ref[...]


def _merge(ya, yb, yc, p, bg, x2d, wa, wb, wc, wo, pg, layer, tm):
    m, d = x2d.shape
    gch = d // V7X_LANES

    zh = NA_HEADS // 2

    def chunks(n, first):
        assert first % n == 0
        return pl.BlockSpec((n, tm, HEAD_DIM), lambda i: (first // n, i, 0))

    def const(shape):
        return _resident(shape, lambda i: (0, 0))

    def weight(w):
        return _resident((None,) + w.shape[1:], lambda i: (layer, 0, 0))

    return pl.pallas_call(
        _merge_kernel,
        grid=(m // tm,),
        in_specs=[
            chunks(NA_HEADS, 0), chunks(GQ_HEADS, 0), chunks(DIL_HEADS_PER_GROUP, 0),
            chunks(zh, C_ZA), chunks(zh, C_ZA + zh), chunks(zh, C_ZB), chunks(zh, C_ZB + zh),
            chunks(DIL_HEADS_PER_GROUP, C_ZC),
            chunks(gch, C_GATE), chunks(gch, C_GATE + gch), chunks(gch, C_GATE + 2 * gch),
            const((1, 3 * d)),
            pl.BlockSpec((tm, d), lambda i: (i, 0)),
            weight(wa), weight(wb), weight(wc), weight(wo),
            const((1, d)),
        ],
        out_specs=pl.BlockSpec((tm, d), lambda i: (i, 0)),
        out_shape=jax.ShapeDtypeStruct((m, d), F32),
        compiler_params=_cparams(("parallel",), 56),
        name="merge",
    )(ya, yb, yc, p, p, p, p, p, p, p, p, bg.reshape(1, 3 * d), x2d, wa, wb, wc, wo, pg.reshape(1, d))


def _layer(x2d, layer, bsz, seq, pre_g, w_in, b_gate, q_g, k_g, na_bias, wa, wb, wc, wo, post_g, cos, sin,
           tables):
    p = _inproj(x2d, pre_g, w_in, layer, tm=min(1024, x2d.shape[0]))
    nprep = (GQ_HEADS + GKV_HEADS) // 2
    gains = jnp.concatenate([jnp.broadcast_to(q_g * QSCALE, (GQ_HEADS // 2, HEAD_DIM)),
                             jnp.broadcast_to(k_g, (GKV_HEADS // 2, HEAD_DIM))], axis=0)
    qk = _qkprep(p, gains.reshape(nprep, 1, HEAD_DIM), cos, sin, seq, tr=min(1024, seq))
    ya = _na(p, na_bias, layer, bsz, seq)
    yb = _gqa(qk, p, bsz, seq, tq=min(256, seq), tk=min(512, seq // GQA_UNROLL),
              ntiles=min(GQA_TILES, seq // min(256, seq)))
    yc = _dil(p, tables, bsz, seq)
    return _merge(ya, yb, yc, p, b_gate, x2d, wa, wb, wc, wo, post_g, layer, tm=256)


def kernel(x, pre_norm_g, w_in, b_gate, q_norm_g, k_norm_g, rpb, w_branch_a, w_branch_b, w_branch_c, w_out,
           post_norm_g):
    bsz, seq, d = x.shape
    depth = w_in.shape[0]
    cos, sin = _rope_tables(seq)
    tables = _dil_tables(seq)
    na_bias = _na_bias(rpb, seq)
    w_in, wa, wb, wc, wo = (w.astype(BF16) for w in (w_in, w_branch_a, w_branch_b, w_branch_c, w_out))
    x2d = x.reshape(bsz * seq, d)
    for l in range(depth):
        x2d = _layer(x2d, l, bsz, seq, pre_norm_g[l], w_in, b_gate[l], q_norm_g[l], k_norm_g[l], na_bias,
                     wa, wb, wc, wo, post_norm_g[l], cos, sin, tables)
    return x2d.reshape(bsz, seq, d)
```

```python
import functools
import math

import numpy as np
import jax
import jax.numpy as jnp
from jax import lax
from jax.experimental import pallas as pl
from jax.experimental.pallas import tpu as pltpu

F32 = jnp.float32
BF16 = jnp.bfloat16

HEAD_DIM = 128
GRID_W = 64
EPS = 1e-6
NEG_INF = -1e30
NA_HEADS = 8
NA_WIN_R = 8
NA_WIN_C = 16
GQ_HEADS = 8
GKV_HEADS = 2
ROPE_BASE = 10000.0
DIL_GROUPS = ((128, 1), (512, 4), (2048, 16))
DIL_HEADS_PER_GROUP = 4
DIL_HEADS = 12
ALIBI_MAX_EXP = 8.0
LOG2E = math.log2(math.e)
QSCALE = HEAD_DIM ** -0.5 * LOG2E

V7X_LANES = 128
MIB = 1024 * 1024

CHUNKS_PER_TILE = 20
SUB_CHUNKS = 4
N_CHUNKS = 140
N_TILES = N_CHUNKS // CHUNKS_PER_TILE
TILE_ROT = 3
C_VC = 0
C_ZA, C_ZB, C_ZC = 12, 20, 28
C_GATE = 32
C_QA, C_KA, C_VA = 80, 88, 96
C_QB, C_KB, C_VB = 104, 112, 114
C_QC, C_KC = 116, 128

NA_R = 4
NA_KW = NA_R + NA_WIN_R - 1
GQA_UNROLL = 4
GQA_TILES = 4
DIL_T = 256
DIL_Q = 4 * DIL_T
DIL_ROWS = 256
DIL_ALIGN = 64


def _cparams(sem, vmem_mib):
    return pltpu.CompilerParams(dimension_semantics=sem, vmem_limit_bytes=int(vmem_mib * MIB))


def _resident(shape, index_map):
    return pl.BlockSpec(shape, index_map, pipeline_mode=pl.Buffered(1))


def _inproj_kernel(x_ref, g_ref, w_ref, p_ref, xn_ref):
    @pl.when(pl.program_id(1) == 0)
    def _():
        x = x_ref[...]
        ms = jnp.mean(x * x, axis=-1, keepdims=True)
        xn_ref[...] = (x * lax.rsqrt(ms + EPS) * g_ref[...]).astype(BF16)

    sub_w = SUB_CHUNKS * V7X_LANES
    for s in range(CHUNKS_PER_TILE // SUB_CHUNKS):
        res = jnp.dot(xn_ref[...], w_ref[:, s * sub_w:(s + 1) * sub_w], preferred_element_type=F32)
        for c in range(SUB_CHUNKS):
            p_ref[s * SUB_CHUNKS + c] = res[:, c * V7X_LANES:(c + 1) * V7X_LANES].astype(BF16)


def _inproj(x2d, g, w, layer, tm):
    m, d = x2d.shape
    tn = CHUNKS_PER_TILE * V7X_LANES
    return pl.pallas_call(
        _inproj_kernel,
        grid=(m // tm, N_TILES),
        in_specs=[
            pl.BlockSpec((tm, d), lambda i, j: (i, 0)),
            pl.BlockSpec((1, d), lambda i, j: (0, 0)),
            pl.BlockSpec((None, d, tn), lambda i, j: (layer, 0, (j + TILE_ROT) % N_TILES)),
        ],
        out_specs=pl.BlockSpec((CHUNKS_PER_TILE, tm, V7X_LANES), lambda i, j: (j, i, 0)),
        out_shape=jax.ShapeDtypeStruct((N_CHUNKS, m, V7X_LANES), BF16),
        scratch_shapes=[pltpu.VMEM((tm, d), BF16)],
        compiler_params=_cparams(("parallel", "arbitrary"), 58),
        name="inproj",
    )(x2d, g.reshape(1, d), w)


def _qkprep_kernel(p_ref, gn_ref, cos_ref, sin_ref, perm_ref, o_ref):
    nc, tr, e = p_ref.shape
    cos = cos_ref[...]
    sin = sin_ref[...]
    gn = gn_ref[0]
    perm = perm_ref[...]
    for c in range(nc):
        x = p_ref[c].astype(F32)
        ms = jnp.mean(x * x, axis=-1, keepdims=True)
        y = x * lax.rsqrt(ms + EPS) * gn
        y_hi = y.astype(BF16)
        y_lo = (y - y_hi.astype(F32)).astype(BF16)
        partner = (jnp.dot(y_hi, perm, preferred_element_type=F32)
                   + jnp.dot(y_lo, perm, preferred_element_type=F32))
        o_ref[c] = (y * cos + partner * sin).astype(BF16)


def _rope_partner_matrix():
    quarter = HEAD_DIM // 4
    j = np.arange(HEAD_DIM)
    src = np.where((j % (2 * quarter)) < quarter, j + quarter, j - quarter)
    perm = np.zeros((HEAD_DIM, HEAD_DIM), np.float32)
    perm[src, j] = 1.0
    return jnp.asarray(perm, BF16)


def _qkprep(p, gains, cos, sin, seq, tr):
    m = p.shape[1]
    nseq = seq // tr
    nblk = (GQ_HEADS + GKV_HEADS) // 2
    return pl.pallas_call(
        _qkprep_kernel,
        grid=(nblk, m // tr),
        in_specs=[
            pl.BlockSpec((2, tr, HEAD_DIM), lambda c, i: (C_QB // 2 + c, i, 0)),
            pl.BlockSpec((1, 1, HEAD_DIM), lambda c, i: (c, 0, 0)),
            pl.BlockSpec((tr, HEAD_DIM), lambda c, i: (i % nseq, 0)),
            pl.BlockSpec((tr, HEAD_DIM), lambda c, i: (i % nseq, 0)),
            pl.BlockSpec((HEAD_DIM, HEAD_DIM), lambda c, i: (0, 0)),
        ],
        out_specs=pl.BlockSpec((2, tr, HEAD_DIM), lambda c, i: (c, i, 0)),
        out_shape=jax.ShapeDtypeStruct((GQ_HEADS + GKV_HEADS, m, HEAD_DIM), BF16),
        compiler_params=_cparams(("parallel", "parallel"), 32),
        name="qkprep",
    )(p, gains, cos, sin, _rope_partner_matrix())


def _rope_tables(seq):
    quarter = HEAD_DIM // 4
    freqs = ROPE_BASE ** (-jnp.arange(quarter, dtype=F32) / quarter)
    t = jnp.arange(seq)
    ang_r = (t // GRID_W).astype(F32)[:, None] * freqs[None, :]
    ang_c = (t % GRID_W).astype(F32)[:, None] * freqs[None, :]
    cos = jnp.concatenate([jnp.cos(ang_r), jnp.cos(ang_r), jnp.cos(ang_c), jnp.cos(ang_c)], axis=-1)
    sin = jnp.concatenate([-jnp.sin(ang_r), jnp.sin(ang_r), -jnp.sin(ang_c), jnp.sin(ang_c)], axis=-1)
    return cos, sin


def _gqa_kernel(q_ref, k_ref, v_ref, o_ref, qt_sc, st0_sc, st1_sc, mx0_sc, mx1_sc, m_sc, l_sc, acc_sc, *, tq, tk):
    grp, tq_step, e = q_ref.shape
    seq = k_ref.shape[1]
    ntiles = tq_step // tq
    nq = grp * tq
    nchunks = seq // tk
    for t in range(ntiles):
        qt_sc[t] = q_ref[:, pl.ds(t * tq, tq), :].reshape(nq, e).T

    def scores(t, i, bufs):
        dst, mx_dst = bufs
        start = pl.multiple_of(i * tk, tk)
        st = jnp.dot(k_ref[0, pl.ds(start, tk), :], qt_sc[t], preferred_element_type=F32)
        dst[...] = st
        mx_dst[...] = jnp.max(st.reshape(tk // 8, 8, nq), axis=0)

    def update(i, bufs):
        src, mx_src = bufs
        start = pl.multiple_of(i * tk, tk)
        vt = v_ref[0, pl.ds(start, tk), :].T
        st = src[...]
        m_old = m_sc[...]
        m_new = jnp.maximum(m_old, jnp.max(mx_src[...], axis=0, keepdims=True))
        alpha = jnp.exp2(m_old - m_new)
        pt = jnp.exp2(st - m_new)
        l_sc[...] = alpha * l_sc[...] + jnp.sum(pt, axis=0, keepdims=True)
        acc_sc[...] = alpha * acc_sc[...] + jnp.dot(vt, pt.astype(BF16), preferred_element_type=F32)
        m_sc[...] = m_new

    bufs = ((st0_sc, mx0_sc), (st1_sc, mx1_sc))
    scores(0, 0, bufs[0])
    for t in range(ntiles):
        m_sc[...] = jnp.full(m_sc.shape, -jnp.inf, F32)
        l_sc[...] = jnp.zeros(l_sc.shape, F32)
        acc_sc[...] = jnp.zeros(acc_sc.shape, F32)

        def quad(j, carry, t=t):
            for u in range(GQA_UNROLL):
                scores(t, GQA_UNROLL * j + u + 1, bufs[(u + 1) % 2])
                update(GQA_UNROLL * j + u, bufs[u % 2])
            return carry

        lax.fori_loop(0, nchunks // GQA_UNROLL - 1, quad, 0)
        base = nchunks - GQA_UNROLL
        for u in range(GQA_UNROLL):
            if u + 1 < GQA_UNROLL:
                scores(t, base + u + 1, bufs[(u + 1) % 2])
            elif t + 1 < ntiles:
                scores(t + 1, 0, bufs[(u + 1) % 2])
            update(base + u, bufs[u % 2])
        o = (acc_sc[...] / l_sc[...]).T
        o_ref[:, pl.ds(t * tq, tq), :] = o.astype(BF16).reshape(grp, tq, e)


def _gqa(qk, p, bsz, seq, tq, tk, ntiles):
    m = qk.shape[1]
    grp = GQ_HEADS // GKV_HEADS
    tq_step = tq * ntiles
    nq = seq // tq_step
    return pl.pallas_call(
        functools.partial(_gqa_kernel, tq=tq, tk=tk),
        grid=(bsz, GKV_HEADS, nq),
        in_specs=[
            pl.BlockSpec((grp, tq_step, HEAD_DIM), lambda b, g, i: (g, b * nq + i, 0)),
            pl.BlockSpec((1, seq, HEAD_DIM), lambda b, g, i: (GQ_HEADS + g, b, 0)),
            pl.BlockSpec((1, seq, HEAD_DIM), lambda b, g, i: (C_VB + g, b, 0)),
        ],
        out_specs=pl.BlockSpec((grp, tq_step, HEAD_DIM), lambda b, g, i: (g, b * nq + i, 0)),
        out_shape=jax.ShapeDtypeStruct((GQ_HEADS, m, HEAD_DIM), BF16),
        scratch_shapes=[pltpu.VMEM((ntiles, HEAD_DIM, grp * tq), BF16),
                        pltpu.VMEM((tk, grp * tq), F32), pltpu.VMEM((tk, grp * tq), F32),
                        pltpu.VMEM((8, grp * tq), F32), pltpu.VMEM((8, grp * tq), F32),
                        pltpu.VMEM((1, grp * tq), F32), pltpu.VMEM((1, grp * tq), F32),
                        pltpu.VMEM((HEAD_DIM, grp * tq), F32)],
        compiler_params=_cparams(("parallel", "parallel", "arbitrary"), 48),
        name="gqa",
    )(qk, qk, p)


def _na_kernel(q_ref, k_ref, v_ref, bias_ref, o_ref, sta_sc, stb_sc, mxa_sc, mxb_sc):
    seq = q_ref.shape[1]
    nq = NA_R * GRID_W
    nkeys = NA_KW * GRID_W
    nblk = seq // nq

    def window(b):
        lead = (NA_WIN_R // 2) * GRID_W
        if isinstance(b, int):
            if b == 0:
                return 0, 0, 0
            if b == nblk - 1:
                return seq - nq, seq - nkeys, 2
            return b * nq, b * nq - lead, 1
        q0 = pl.multiple_of(b * nq, nq)
        return q0, pl.multiple_of(q0 - lead, GRID_W), 1

    def scores(b, bufs):
        dst, mx_dst = bufs
        q0, k0, variant = window(b)
        qt = (q_ref[0, pl.ds(q0, nq), :].astype(F32) * QSCALE).astype(BF16).T
        k = k_ref[0, pl.ds(k0, nkeys), :]
        st = jnp.dot(k, qt, preferred_element_type=F32) + bias_ref[variant, 0]
        dst[...] = st
        mx_dst[...] = jnp.max(st.reshape(nkeys // 8, 8, nq), axis=0)

    def update(b, bufs):
        src, mx_src = bufs
        q0, k0, _ = window(b)
        st = src[...]
        vt = v_ref[0, pl.ds(k0, nkeys), :].T
        pt = jnp.exp2(st - jnp.max(mx_src[...], axis=0, keepdims=True))
        l = jnp.sum(pt, axis=0, keepdims=True)
        ot = jnp.dot(vt, pt.astype(BF16), preferred_element_type=F32)
        o_ref[0, pl.ds(q0, nq), :] = (ot / l).T.astype(BF16)

    buf_a, buf_b = (sta_sc, mxa_sc), (stb_sc, mxb_sc)
    scores(0, buf_a)
    scores(1, buf_b)
    update(0, buf_a)

    def pair(j, carry):
        scores(2 * j, buf_a)
        update(2 * j - 1, buf_b)
        scores(2 * j + 1, buf_b)
        update(2 * j, buf_a)
        return carry

    lax.fori_loop(1, nblk // 2 - 1, pair, 0)
    scores(nblk - 2, buf_a)
    update(nblk - 3, buf_b)
    scores(nblk - 1, buf_b)
    update(nblk - 2, buf_a)
    update(nblk - 1, buf_b)


def _na(p, bias, layer, bsz, seq):
    m = p.shape[1]
    nq = NA_R * GRID_W
    nkeys = NA_KW * GRID_W
    return pl.pallas_call(
        _na_kernel,
        grid=(NA_HEADS, bsz),
        in_specs=[
            pl.BlockSpec((1, seq, HEAD_DIM), lambda h, b: (C_QA + h, b, 0)),
            pl.BlockSpec((1, seq, HEAD_DIM), lambda h, b: (C_KA + h, b, 0)),
            pl.BlockSpec((1, seq, HEAD_DIM), lambda h, b: (C_VA + h, b, 0)),
            pl.BlockSpec((None, 3, 1, nkeys, nq), lambda h, b: (layer, 0, h, 0, 0)),
        ],
        out_specs=pl.BlockSpec((1, seq, HEAD_DIM), lambda h, b: (h, b, 0)),
        out_shape=jax.ShapeDtypeStruct((NA_HEADS, m, HEAD_DIM), BF16),
        scratch_shapes=[pltpu.VMEM((nkeys, nq), F32), pltpu.VMEM((nkeys, nq), F32),
                        pltpu.VMEM((8, nq), F32), pltpu.VMEM((8, nq), F32)],
        compiler_params=_cparams(("parallel", "parallel"), 48),
        name="na",
    )(p, p, p, bias)


def _na_bias(rpb, seq):
    rows = seq // GRID_W
    kr = min(NA_WIN_R, rows)
    ndr = 2 * NA_WIN_R - 1
    ndc = 2 * NA_WIN_C - 1
    j = np.arange(GRID_W)[:, None]
    c = np.arange(GRID_W)[None, :]
    cs = np.clip(j - NA_WIN_C // 2, 0, GRID_W - NA_WIN_C)
    col_ok = (c >= cs) & (c < cs + NA_WIN_C)
    dc = np.clip(c - j + NA_WIN_C - 1, 0, ndc - 1)
    onehot = (dc[:, :, None] == np.arange(ndc)[None, None, :]).astype(np.float32)
    tiles = jnp.einsum('lhrd,jcd->lhrjc', rpb.astype(F32), jnp.asarray(onehot),
                       precision=lax.Precision.HIGHEST)
    tiles = jnp.where(jnp.asarray(col_ok), tiles * LOG2E, NEG_INF)
    masked = jnp.full(tiles.shape[:2] + (1, GRID_W, GRID_W), NEG_INF, F32)
    tiles = jnp.concatenate([tiles, masked], axis=2)
    variants = []
    for r0, k0 in ((0, 0), (NA_R, NA_R - kr // 2), (rows - NA_R, rows - NA_KW)):
        r = r0 + np.arange(NA_R)[:, None]
        krow = k0 + np.arange(NA_KW)[None, :]
        start = np.clip(r - kr // 2, 0, rows - kr)
        row_ok = (krow >= start) & (krow < start + kr)
        dr = np.where(row_ok, krow - r + NA_WIN_R - 1, ndr)
        sel = jnp.concatenate([tiles[:, :, d:d + 1] for d in dr.T.reshape(-1)], axis=2)
        sel = sel.reshape(sel.shape[:2] + (NA_KW, NA_R, GRID_W, GRID_W))
        sel = sel.transpose(0, 1, 2, 5, 3, 4)
        variants.append(sel.reshape(sel.shape[:2] + (NA_KW * GRID_W, NA_R * GRID_W)))
    return jnp.stack(variants, axis=1)


def _dil_windows(seq):
    out = []
    for win, _ in DIL_GROUPS:
        hw = -(-(win // 2) // DIL_ALIGN) * DIL_ALIGN
        out.append((hw, min(2 * hw + DIL_T, seq)))
    return out


def _dil_kernel(q0_ref, q1_ref, q2_ref, k0_ref, k1_ref, k2_ref, v0_ref, v1_ref, v2_ref,
                e0_ref, e1_ref, e2_ref, o_ref, sta_sc, stb_sc, mxa_sc, mxb_sc, pt_sc):
    q_refs = (q0_ref, q1_ref, q2_ref)
    k_refs = (k0_ref, k1_ref, k2_ref)
    v_refs = (v0_ref, v1_ref, v2_ref)
    e_refs = (e0_ref, e1_ref, e2_ref)
    seq = k0_ref.shape[1]
    nsub = q0_ref.shape[1] // DIL_T
    windows = _dil_windows(seq)
    offs = [sum(w for _, w in windows[:g]) for g in range(len(windows))]

    def key_start(sub, g):
        t0 = pl.program_id(2) * q0_ref.shape[1] + sub * DIL_T
        hw, wlen = windows[g]
        ws = pl.multiple_of(jnp.clip(t0 - hw, 0, seq - wlen), DIL_ALIGN)
        return ws, pl.multiple_of(wlen - DIL_T - (t0 - ws), DIL_ALIGN)

    def scores(sub, dst, mx_dst):
        m8 = None
        for g, (hw, wlen) in enumerate(windows):
            ws, erow = key_start(sub, g)
            qt = (q_refs[g][0, pl.ds(sub * DIL_T, DIL_T), :].astype(F32) * QSCALE).astype(BF16).T
            for r in range(0, wlen, DIL_ROWS):
                n = min(DIL_ROWS, wlen - r)
                k = k_refs[g][0, pl.ds(ws + r, n), :]
                st = (jnp.dot(k, qt, preferred_element_type=F32)
                      + e_refs[g][0, pl.ds(erow + r, n), :])
                dst[pl.ds(offs[g] + r, n), :] = st
                m = jnp.max(st.reshape(n // 8, 8, DIL_T), axis=0)
                m8 = m if m8 is None else jnp.maximum(m8, m)
        mx_dst[...] = m8

    def update(sub, src, mx_src):
        mx = jnp.max(mx_src[...], axis=0, keepdims=True)
        nkeys = src.shape[0]
        l8 = jnp.zeros((8, DIL_T), F32)
        for r in range(0, nkeys, DIL_ROWS):
            n = min(DIL_ROWS, nkeys - r)
            pt = jnp.exp2(src[pl.ds(r, n), :] - mx)
            l8 = l8 + jnp.sum(pt.reshape(n // 8, 8, DIL_T), axis=0)
            pt_sc[pl.ds(r, n), :] = pt.astype(BF16)
        l = jnp.sum(l8, axis=0, keepdims=True)
        acc = jnp.zeros((HEAD_DIM, DIL_T), F32)
        for g, (hw, wlen) in enumerate(windows):
            ws, _ = key_start(sub, g)
            vt = v_refs[g][0, pl.ds(ws, wlen), :].T
            acc = acc + jnp.dot(vt, pt_sc[pl.ds(offs[g], wlen), :], preferred_element_type=F32)
        o_ref[0, pl.ds(sub * DIL_T, DIL_T), :] = (acc / l).T.astype(BF16)

    bufs = ((sta_sc, mxa_sc), (stb_sc, mxb_sc))
    scores(0, *bufs[0])
    for sub in range(nsub):
        if sub + 1 < nsub:
            scores(sub + 1, *bufs[(sub + 1) % 2])
        update(sub, *bufs[sub % 2])


def _dil(p, tables, bsz, seq):
    m = p.shape[1]
    tq = min(DIL_Q, seq)
    nq = seq // tq
    hg = DIL_HEADS_PER_GROUP
    nkeys = sum(w for _, w in _dil_windows(seq))

    def qspec(g):
        return pl.BlockSpec((1, tq, HEAD_DIM), lambda b, h, i: (C_QC + hg * g + h, b * nq + i, 0))

    def kvspec(base, g):
        return _resident((1, seq, HEAD_DIM), lambda b, h, i: (base + hg * g + h, b, 0))

    def tspec(t):
        return _resident((1,) + t.shape[1:], lambda b, h, i: (h, 0, 0))

    return pl.pallas_call(
        _dil_kernel,
        grid=(bsz, hg, nq),
        in_specs=[qspec(0), qspec(1), qspec(2),
                  kvspec(C_KC, 0), kvspec(C_KC, 1), kvspec(C_KC, 2),
                  kvspec(C_VC, 0), kvspec(C_VC, 1), kvspec(C_VC, 2),
                  tspec(tables[0]), tspec(tables[1]), tspec(tables[2])],
        out_specs=pl.BlockSpec((1, tq, HEAD_DIM), lambda b, h, i: (h, b * nq + i, 0)),
        out_shape=jax.ShapeDtypeStruct((hg, m, HEAD_DIM), BF16),
        scratch_shapes=[pltpu.VMEM((nkeys, DIL_T), F32), pltpu.VMEM((nkeys, DIL_T), F32),
                        pltpu.VMEM((8, DIL_T), F32), pltpu.VMEM((8, DIL_T), F32),
                        pltpu.VMEM((nkeys, DIL_T), BF16)],
        compiler_params=_cparams(("parallel", "parallel", "arbitrary"), 56),
        name="dilated",
    )(p, p, p, p, p, p, p, p, p, *tables)


def _dil_tables(seq):
    slopes = 2.0 ** (-ALIBI_MAX_EXP * jnp.arange(1, DIL_HEADS + 1, dtype=F32) / DIL_HEADS)
    out = []
    for g, (hw, wlen) in enumerate(_dil_windows(seq)):
        win, dil = DIL_GROUPS[g]
        r = np.arange(2 * wlen - DIL_T)[:, None]
        i = np.arange(DIL_T)[None, :]
        rel = i - r + wlen - DIL_T
        ok = jnp.asarray((np.abs(rel) <= win // 2) & (rel % dil == 0))
        dist = jnp.asarray(np.abs(rel), F32)
        sl = slopes[DIL_HEADS_PER_GROUP * g:DIL_HEADS_PER_GROUP * (g + 1)] * LOG2E
        out.append(jnp.where(ok[None], -sl[:, None, None] * dist[None], NEG_INF))
    return out


def _merge_kernel(ya_ref, yb_ref, yc_ref, za0_ref, za1_ref, zb0_ref, zb1_ref, zc_ref, ga_ref, gb_ref, gc_ref,
                  bg_ref, x_ref, wa_ref, wb_ref, wc_ref, wo_ref, pg_ref, o_ref):
    d = x_ref.shape[1]

    def cat(*refs):
        return jnp.concatenate([r[c] for r in refs for c in range(r.shape[0])], axis=-1).astype(F32)

    def branch(y_ref, z_refs, w_ref):
        z = cat(*z_refs)
        u = (cat(y_ref) * (z * jax.nn.sigmoid(z))).astype(BF16)
        return jnp.dot(u, w_ref[...], preferred_element_type=F32)

    def gate(g_ref, k):
        return jax.nn.sigmoid(cat(g_ref) + bg_ref[:, k * d:(k + 1) * d])

    merged = (gate(ga_ref, 0) * branch(ya_ref, (za0_ref, za1_ref), wa_ref)
              + gate(gb_ref, 1) * branch(yb_ref, (zb0_ref, zb1_ref), wb_ref)
              + gate(gc_ref, 2) * branch(yc_ref, (zc_ref,), wc_ref))
    out = jnp.dot(merged.astype(BF16), wo_ref[...], preferred_element_type=F32)
    ms = jnp.mean(out * out, axis=-1, keepdims=True)
    o_ref[...] = x_ref[...] + out * lax.rsqrt(ms + EPS) * pg_ref[...]


def _merge(ya, yb, yc, p, bg, x2d, wa, wb, wc, wo, pg, layer, tm):
    m, d = x2d.shape
    gch = d // V7X_LANES

    zh = NA_HEADS // 2

    def chunks(n, first):
        assert first % n == 0
        return pl.BlockSpec((n, tm, HEAD_DIM), lambda i: (first // n, i, 0))

    def const(shape):
        return _resident(shape, lambda i: (0, 0))

    def weight(w):
        return _resident((None,) + w.shape[1:], lambda i: (layer, 0, 0))

    return pl.pallas_call(
        _merge_kernel,
        grid=(m // tm,),
        in_specs=[
            chunks(NA_HEADS, 0), chunks(GQ_HEADS, 0), chunks(DIL_HEADS_PER_GROUP, 0),
            chunks(zh, C_ZA), chunks(zh, C_ZA + zh), chunks(zh, C_ZB), chunks(zh, C_ZB + zh),
            chunks(DIL_HEADS_PER_GROUP, C_ZC),
            chunks(gch, C_GATE), chunks(gch, C_GATE + gch), chunks(gch, C_GATE + 2 * gch),
            const((1, 3 * d)),
            pl.BlockSpec((tm, d), lambda i: (i, 0)),
            weight(wa), weight(wb), weight(wc), weight(wo),
            const((1, d)),
        ],
        out_specs=pl.BlockSpec((tm, d), lambda i: (i, 0)),
        out_shape=jax.ShapeDtypeStruct((m, d), F32),
        compiler_params=_cparams(("parallel",), 56),
        name="merge",
    )(ya, yb, yc, p, p, p, p, p, p, p, p, bg.reshape(1, 3 * d), x2d, wa, wb, wc, wo, pg.reshape(1, d))


def _layer(x2d, layer, bsz, seq, pre_g, w_in, b_gate, q_g, k_g, na_bias, wa, wb, wc, wo, post_g, cos, sin,
           tables):
    p = _inproj(x2d, pre_g, w_in, layer, tm=min(1024, x2d.shape[0]))
    nprep = (GQ_HEADS + GKV_HEADS) // 2
    gains = jnp.concatenate([jnp.broadcast_to(q_g * QSCALE, (GQ_HEADS // 2, HEAD_DIM)),
                             jnp.broadcast_to(k_g, (GKV_HEADS // 2, HEAD_DIM))], axis=0)
    qk = _qkprep(p, gains.reshape(nprep, 1, HEAD_DIM), cos, sin, seq, tr=min(1024, seq))
    ya = _na(p, na_bias, layer, bsz, seq)
    yb = _gqa(qk, p, bsz, seq, tq=min(256, seq), tk=min(512, seq // GQA_UNROLL),
              ntiles=min(GQA_TILES, seq // min(256, seq)))
    yc = _dil(p, tables, bsz, seq)
    return _merge(ya, yb, yc, p, b_gate, x2d, wa, wb, wc, wo, post_g, layer, tm=256)


def kernel(x, pre_norm_g, w_in, b_gate, q_norm_g, k_norm_g, rpb, w_branch_a, w_branch_b, w_branch_c, w_out,
           post_norm_g):
    bsz, seq, d = x.shape
    depth = w_in.shape[0]
    cos, sin = _rope_tables(seq)
    tables = _dil_tables(seq)
    na_bias = _na_bias(rpb, seq)
    w_in, wa, wb, wc, wo = (w.astype(BF16) for w in (w_in, w_branch_a, w_branch_b, w_branch_c, w_out))
    x2d = x.reshape(bsz * seq, d)
    for l in range(depth):
        x2d = _layer(x2d, l, bsz, seq, pre_norm_g[l], w_in, b_gate[l], q_norm_g[l], k_norm_g[l], na_bias,
                     wa, wb, wc, wo, post_norm_g[l], cos, sin, tables)
    return x2d.reshape(bsz, seq, d)
```

```python
import functools
import math

import numpy as np
import jax
import jax.numpy as jnp
from jax import lax
from jax.experimental import pallas as pl
from jax.experimental.pallas import tpu as pltpu

F32 = jnp.float32
BF16 = jnp.bfloat16

HEAD_DIM = 128
GRID_W = 64
EPS = 1e-6
NEG_INF = -1e30
NA_HEADS = 8
NA_WIN_R = 8
NA_WIN_C = 16
GQ_HEADS = 8
GKV_HEADS = 2
ROPE_BASE = 10000.0
DIL_GROUPS = ((128, 1), (512, 4), (2048, 16))
DIL_HEADS_PER_GROUP = 4
DIL_HEADS = 12
ALIBI_MAX_EXP = 8.0
LOG2E = math.log2(math.e)
QSCALE = HEAD_DIM ** -0.5 * LOG2E

V7X_LANES = 128
MIB = 1024 * 1024

CHUNKS_PER_TILE = 20
SUB_CHUNKS = 4
N_CHUNKS = 140
N_TILES = N_CHUNKS // CHUNKS_PER_TILE
TILE_ROT = 3
C_VC = 0
C_ZA, C_ZB, C_ZC = 12, 20, 28
C_GATE = 32
C_QA, C_KA, C_VA = 80, 88, 96
C_QB, C_KB, C_VB = 104, 112, 114
C_QC, C_KC = 116, 128

NA_R = 4
NA_KW = NA_R + NA_WIN_R - 1
GQA_UNROLL = 4
GQA_TILES = 4
DIL_T = 256
DIL_Q = 4 * DIL_T
DIL_ROWS = 256
DIL_ALIGN = 64


def _cparams(sem, vmem_mib):
    return pltpu.CompilerParams(dimension_semantics=sem, vmem_limit_bytes=int(vmem_mib * MIB))


def _resident(shape, index_map):
    return pl.BlockSpec(shape, index_map, pipeline_mode=pl.Buffered(1))


def _inproj_kernel(x_ref, g_ref, w_ref, p_ref, xn_ref):
    @pl.when(pl.program_id(1) == 0)
    def _():
        x = x_ref[...]
        ms = jnp.mean(x * x, axis=-1, keepdims=True)
        xn_ref[...] = (x * lax.rsqrt(ms + EPS) * g_ref[...]).astype(BF16)

    sub_w = SUB_CHUNKS * V7X_LANES
    for s in range(CHUNKS_PER_TILE // SUB_CHUNKS):
        res = jnp.dot(xn_ref[...], w_ref[:, s * sub_w:(s + 1) * sub_w], preferred_element_type=F32)
        for c in range(SUB_CHUNKS):
            p_ref[s * SUB_CHUNKS + c] = res[:, c * V7X_LANES:(c + 1) * V7X_LANES].astype(BF16)


def _inproj(x2d, g, w, layer, tm):
    m, d = x2d.shape
    tn = CHUNKS_PER_TILE * V7X_LANES
    return pl.pallas_call(
        _inproj_kernel,
        grid=(m // tm, N_TILES),
        in_specs=[
            pl.BlockSpec((tm, d), lambda i, j: (i, 0)),
            pl.BlockSpec((1, d), lambda i, j: (0, 0)),
            pl.BlockSpec((None, d, tn), lambda i, j: (layer, 0, (j + TILE_ROT) % N_TILES)),
        ],
        out_specs=pl.BlockSpec((CHUNKS_PER_TILE, tm, V7X_LANES), lambda i, j: (j, i, 0)),
        out_shape=jax.ShapeDtypeStruct((N_CHUNKS, m, V7X_LANES), BF16),
        scratch_shapes=[pltpu.VMEM((tm, d), BF16)],
        compiler_params=_cparams(("parallel", "arbitrary"), 58),
        name="inproj",
    )(x2d, g.reshape(1, d), w)


def _qkprep_kernel(p_ref, gn_ref, cos_ref, sin_ref, perm_ref, o_ref):
    nc, tr, e = p_ref.shape
    cos = cos_ref[...]
    sin = sin_ref[...]
    gn = gn_ref[0]
    perm = perm_ref[...]
    for c in range(nc):
        x = p_ref[c].astype(F32)
        ms = jnp.mean(x * x, axis=-1, keepdims=True)
        y = x * lax.rsqrt(ms + EPS) * gn
        y_hi = y.astype(BF16)
        y_lo = (y - y_hi.astype(F32)).astype(BF16)
        partner = (jnp.dot(y_hi, perm, preferred_element_type=F32)
                   + jnp.dot(y_lo, perm, preferred_element_type=F32))
        o_ref[c] = (y * cos + partner * sin).astype(BF16)


def _rope_partner_matrix():
    quarter = HEAD_DIM // 4
    j = np.arange(HEAD_DIM)
    src = np.where((j % (2 * quarter)) < quarter, j + quarter, j - quarter)
    perm = np.zeros((HEAD_DIM, HEAD_DIM), np.float32)
    perm[src, j] = 1.0
    return jnp.asarray(perm, BF16)


def _qkprep(p, gains, cos, sin, seq, tr):
    m = p.shape[1]
    nseq = seq // tr
    nblk = (GQ_HEADS + GKV_HEADS) // 2
    return pl.pallas_call(
        _qkprep_kernel,
        grid=(nblk, m // tr),
        in_specs=[
            pl.BlockSpec((2, tr, HEAD_DIM), lambda c, i: (C_QB // 2 + c, i, 0)),
            pl.BlockSpec((1, 1, HEAD_DIM), lambda c, i: (c, 0, 0)),
            pl.BlockSpec((tr, HEAD_DIM), lambda c, i: (i % nseq, 0)),
            pl.BlockSpec((tr, HEAD_DIM), lambda c, i: (i % nseq, 0)),
            pl.BlockSpec((HEAD_DIM, HEAD_DIM), lambda c, i: (0, 0)),
        ],
        out_specs=pl.BlockSpec((2, tr, HEAD_DIM), lambda c, i: (c, i, 0)),
        out_shape=jax.ShapeDtypeStruct((GQ_HEADS + GKV_HEADS, m, HEAD_DIM), BF16),
        compiler_params=_cparams(("parallel", "parallel"), 32),
        name="qkprep",
    )(p, gains, cos, sin, _rope_partner_matrix())


def _rope_tables(seq):
    quarter = HEAD_DIM // 4
    freqs = ROPE_BASE ** (-jnp.arange(quarter, dtype=F32) / quarter)
    t = jnp.arange(seq)
    ang_r = (t // GRID_W).astype(F32)[:, None] * freqs[None, :]
    ang_c = (t % GRID_W).astype(F32)[:, None] * freqs[None, :]
    cos = jnp.concatenate([jnp.cos(ang_r), jnp.cos(ang_r), jnp.cos(ang_c), jnp.cos(ang_c)], axis=-1)
    sin = jnp.concatenate([-jnp.sin(ang_r), jnp.sin(ang_r), -jnp.sin(ang_c), jnp.sin(ang_c)], axis=-1)
    return cos, sin


def _gqa_kernel(q_ref, k_ref, v_ref, o_ref, qt_sc, st0_sc, st1_sc, mx0_sc, mx1_sc, m_sc, l_sc, acc_sc, *, tq, tk):
    grp, tq_step, e = q_ref.shape
    seq = k_ref.shape[1]
    ntiles = tq_step // tq
    nq = grp * tq
    nchunks = seq // tk
    for t in range(ntiles):
        qt_sc[t] = q_ref[:, pl.ds(t * tq, tq), :].reshape(nq, e).T

    def scores(t, i, bufs):
        dst, mx_dst = bufs
        start = pl.multiple_of(i * tk, tk)
        st = jnp.dot(k_ref[0, pl.ds(start, tk), :], qt_sc[t], preferred_element_type=F32)
        dst[...] = st
        mx_dst[...] = jnp.max(st.reshape(tk // 8, 8, nq), axis=0)

    def update(i, bufs):
        src, mx_src = bufs
        start = pl.multiple_of(i * tk, tk)
        vt = v_ref[0, pl.ds(start, tk), :].T
        st = src[...]
        m_old = m_sc[...]
        m_new = jnp.maximum(m_old, jnp.max(mx_src[...], axis=0, keepdims=True))
        alpha = jnp.exp2(m_old - m_new)
        pt = jnp.exp2(st - m_new)
        l_sc[...] = alpha * l_sc[...] + jnp.sum(pt, axis=0, keepdims=True)
        acc_sc[...] = alpha * acc_sc[...] + jnp.dot(vt, pt.astype(BF16), preferred_element_type=F32)
        m_sc[...] = m_new

    bufs = ((st0_sc, mx0_sc), (st1_sc, mx1_sc))
    scores(0, 0, bufs[0])
    for t in range(ntiles):
        m_sc[...] = jnp.full(m_sc.shape, -jnp.inf, F32)
        l_sc[...] = jnp.zeros(l_sc.shape, F32)
        acc_sc[...] = jnp.zeros(acc_sc.shape, F32)

        def quad(j, carry, t=t):
            for u in range(GQA_UNROLL):
                scores(t, GQA_UNROLL * j + u + 1, bufs[(u + 1) % 2])
                update(GQA_UNROLL * j + u, bufs[u % 2])
            return carry

        lax.fori_loop(0, nchunks // GQA_UNROLL - 1, quad, 0)
        base = nchunks - GQA_UNROLL
        for u in range(GQA_UNROLL):
            if u + 1 < GQA_UNROLL:
                scores(t, base + u + 1, bufs[(u + 1) % 2])
            elif t + 1 < ntiles:
                scores(t + 1, 0, bufs[(u + 1) % 2])
            update(base + u, bufs[u % 2])
        o = (acc_sc[...] / l_sc[...]).T
        o_ref[:, pl.ds(t * tq, tq), :] = o.astype(BF16).reshape(grp, tq, e)


def _gqa(qk, p, bsz, seq, tq, tk, ntiles):
    m = qk.shape[1]
    grp = GQ_HEADS // GKV_HEADS
    tq_step = tq * ntiles
    nq = seq // tq_step
    return pl.pallas_call(
        functools.partial(_gqa_kernel, tq=tq, tk=tk),
        grid=(bsz, GKV_HEADS, nq),
        in_specs=[
            pl.BlockSpec((grp, tq_step, HEAD_DIM), lambda b, g, i: (g, b * nq + i, 0)),
            pl.BlockSpec((1, seq, HEAD_DIM), lambda b, g, i: (GQ_HEADS + g, b, 0)),
            pl.BlockSpec((1, seq, HEAD_DIM), lambda b, g, i: (C_VB + g, b, 0)),
        ],
        out_specs=pl.BlockSpec((grp, tq_step, HEAD_DIM), lambda b, g, i: (g, b * nq + i, 0)),
        out_shape=jax.ShapeDtypeStruct((GQ_HEADS, m, HEAD_DIM), BF16),
        scratch_shapes=[pltpu.VMEM((ntiles, HEAD_DIM, grp * tq), BF16),
                        pltpu.VMEM((tk, grp * tq), F32), pltpu.VMEM((tk, grp * tq), F32),
                        pltpu.VMEM((8, grp * tq), F32), pltpu.VMEM((8, grp * tq), F32),
                        pltpu.VMEM((1, grp * tq), F32), pltpu.VMEM((1, grp * tq), F32),
                        pltpu.VMEM((HEAD_DIM, grp * tq), F32)],
        compiler_params=_cparams(("parallel", "parallel", "arbitrary"), 48),
        name="gqa",
    )(qk, qk, p)


def _na_kernel(q_ref, k_ref, v_ref, bias_ref, o_ref, sta_sc, stb_sc, mxa_sc, mxb_sc):
    seq = q_ref.shape[1]
    nq = NA_R * GRID_W
    nkeys = NA_KW * GRID_W
    nblk = seq // nq

    def window(b):
        lead = (NA_WIN_R // 2) * GRID_W
        if isinstance(b, int):
            if b == 0:
                return 0, 0, 0
            if b == nblk - 1:
                return seq - nq, seq - nkeys, 2
            return b * nq, b * nq - lead, 1
        q0 = pl.multiple_of(b * nq, nq)
        return q0, pl.multiple_of(q0 - lead, GRID_W), 1

    def scores(b, bufs):
        dst, mx_dst = bufs
        q0, k0, variant = window(b)
        qt = (q_ref[0, pl.ds(q0, nq), :].astype(F32) * QSCALE).astype(BF16).T
        k = k_ref[0, pl.ds(k0, nkeys), :]
        st = jnp.dot(k, qt, preferred_element_type=F32) + bias_ref[variant, 0].astype(F32)
        dst[...] = st
        mx_dst[...] = jnp.max(st.reshape(nkeys // 8, 8, nq), axis=0)

    def update(b, bufs):
        src, mx_src = bufs
        q0, k0, _ = window(b)
        st = src[...]
        vt = v_ref[0, pl.ds(k0, nkeys), :].T
        pt = jnp.exp2(st - jnp.max(mx_src[...], axis=0, keepdims=True))
        l = jnp.sum(pt, axis=0, keepdims=True)
        ot = jnp.dot(vt, pt.astype(BF16), preferred_element_type=F32)
        o_ref[0, pl.ds(q0, nq), :] = (ot / l).T.astype(BF16)

    buf_a, buf_b = (sta_sc, mxa_sc), (stb_sc, mxb_sc)
    scores(0, buf_a)
    scores(1, buf_b)
    update(0, buf_a)

    def pair(j, carry):
        scores(2 * j, buf_a)
        update(2 * j - 1, buf_b)
        scores(2 * j + 1, buf_b)
        update(2 * j, buf_a)
        return carry

    lax.fori_loop(1, nblk // 2 - 1, pair, 0)
    scores(nblk - 2, buf_a)
    update(nblk - 3, buf_b)
    scores(nblk - 1, buf_b)
    update(nblk - 2, buf_a)
    update(nblk - 1, buf_b)


def _na(p, bias, layer, bsz, seq):
    m = p.shape[1]
    nq = NA_R * GRID_W
    nkeys = NA_KW * GRID_W
    return pl.pallas_call(
        _na_kernel,
        grid=(NA_HEADS, bsz),
        in_specs=[
            pl.BlockSpec((1, seq, HEAD_DIM), lambda h, b: (C_QA + h, b, 0)),
            pl.BlockSpec((1, seq, HEAD_DIM), lambda h, b: (C_KA + h, b, 0)),
            pl.BlockSpec((1, seq, HEAD_DIM), lambda h, b: (C_VA + h, b, 0)),
            pl.BlockSpec((None, 3, 1, nkeys, nq), lambda h, b: (layer, 0, h, 0, 0)),
        ],
        out_specs=pl.BlockSpec((1, seq, HEAD_DIM), lambda h, b: (h, b, 0)),
        out_shape=jax.ShapeDtypeStruct((NA_HEADS, m, HEAD_DIM), BF16),
        scratch_shapes=[pltpu.VMEM((nkeys, nq), F32), pltpu.VMEM((nkeys, nq), F32),
                        pltpu.VMEM((8, nq), F32), pltpu.VMEM((8, nq), F32)],
        compiler_params=_cparams(("parallel", "parallel"), 48),
        name="na",
    )(p, p, p, bias)


def _na_bias(rpb, seq):
    rows = seq // GRID_W
    kr = min(NA_WIN_R, rows)
    ndr = 2 * NA_WIN_R - 1
    ndc = 2 * NA_WIN_C - 1
    c = np.arange(GRID_W)[:, None]
    j = np.arange(GRID_W)[None, :]
    cs = np.clip(j - NA_WIN_C // 2, 0, GRID_W - NA_WIN_C)
    col_ok = (c >= cs) & (c < cs + NA_WIN_C)
    dc = np.clip(c - j + NA_WIN_C - 1, 0, ndc - 1)
    onehot = (dc[:, :, None] == np.arange(ndc)[None, None, :]).astype(np.float32)
    tiles = jnp.einsum('lhrd,cjd->lhrcj', rpb.astype(F32), jnp.asarray(onehot),
                       precision=lax.Precision.HIGHEST)
    tiles = jnp.where(jnp.asarray(col_ok), tiles * LOG2E, NEG_INF).astype(BF16)
    masked = jnp.full(tiles.shape[:2] + (1, GRID_W, GRID_W), NEG_INF, BF16)
    tiles = jnp.concatenate([tiles, masked], axis=2)
    index = []
    for r0, k0 in ((0, 0), (NA_R, NA_R - kr // 2), (rows - NA_R, rows - NA_KW)):
        r = r0 + np.arange(NA_R)[None, :]
        krow = k0 + np.arange(NA_KW)[:, None]
        start = np.clip(r - kr // 2, 0, rows - kr)
        row_ok = (krow >= start) & (krow < start + kr)
        index.append(np.where(row_ok, krow - r + NA_WIN_R - 1, ndr))
    index = np.stack(index).reshape(-1)
    sel = jnp.take(tiles, jnp.asarray(index), axis=2)
    sel = sel.reshape(sel.shape[:2] + (3, NA_KW, NA_R, GRID_W, GRID_W))
    sel = sel.transpose(0, 2, 1, 3, 5, 4, 6)
    return sel.reshape(sel.shape[:3] + (NA_KW * GRID_W, NA_R * GRID_W))


def _dil_windows(seq):
    out = []
    for win, _ in DIL_GROUPS:
        hw = -(-(win // 2) // DIL_ALIGN) * DIL_ALIGN
        out.append((hw, min(2 * hw + DIL_T, seq)))
    return out


def _dil_kernel(q0_ref, q1_ref, q2_ref, k0_ref, k1_ref, k2_ref, v0_ref, v1_ref, v2_ref,
                e0_ref, e1_ref, e2_ref, o_ref, sta_sc, stb_sc, mxa_sc, mxb_sc, pt_sc):
    q_refs = (q0_ref, q1_ref, q2_ref)
    k_refs = (k0_ref, k1_ref, k2_ref)
    v_refs = (v0_ref, v1_ref, v2_ref)
    e_refs = (e0_ref, e1_ref, e2_ref)
    seq = k0_ref.shape[1]
    nsub = q0_ref.shape[1] // DIL_T
    windows = _dil_windows(seq)
    offs = [sum(w for _, w in windows[:g]) for g in range(len(windows))]

    def key_start(sub, g):
        t0 = pl.program_id(2) * q0_ref.shape[1] + sub * DIL_T
        hw, wlen = windows[g]
        ws = pl.multiple_of(jnp.clip(t0 - hw, 0, seq - wlen), DIL_ALIGN)
        return ws, pl.multiple_of(wlen - DIL_T - (t0 - ws), DIL_ALIGN)

    def scores(sub, dst, mx_dst):
        m8 = None
        for g, (hw, wlen) in enumerate(windows):
            ws, erow = key_start(sub, g)
            qt = (q_refs[g][0, pl.ds(sub * DIL_T, DIL_T), :].astype(F32) * QSCALE).astype(BF16).T
            for r in range(0, wlen, DIL_ROWS):
                n = min(DIL_ROWS, wlen - r)
                k = k_refs[g][0, pl.ds(ws + r, n), :]
                st = (jnp.dot(k, qt, preferred_element_type=F32)
                      + e_refs[g][0, pl.ds(erow + r, n), :])
                dst[pl.ds(offs[g] + r, n), :] = st
                m = jnp.max(st.reshape(n // 8, 8, DIL_T), axis=0)
                m8 = m if m8 is None else jnp.maximum(m8, m)
        mx_dst[...] = m8

    def update(sub, src, mx_src):
        mx = jnp.max(mx_src[...], axis=0, keepdims=True)
        nkeys = src.shape[0]
        l8 = jnp.zeros((8, DIL_T), F32)
        for r in range(0, nkeys, DIL_ROWS):
            n = min(DIL_ROWS, nkeys - r)
            pt = jnp.exp2(src[pl.ds(r, n), :] - mx)
            l8 = l8 + jnp.sum(pt.reshape(n // 8, 8, DIL_T), axis=0)
            pt_sc[pl.ds(r, n), :] = pt.astype(BF16)
        l = jnp.sum(l8, axis=0, keepdims=True)
        acc = jnp.zeros((HEAD_DIM, DIL_T), F32)
        for g, (hw, wlen) in enumerate(windows):
            ws, _ = key_start(sub, g)
            vt = v_refs[g][0, pl.ds(ws, wlen), :].T
            acc = acc + jnp.dot(vt, pt_sc[pl.ds(offs[g], wlen), :], preferred_element_type=F32)
        o_ref[0, pl.ds(sub * DIL_T, DIL_T), :] = (acc / l).T.astype(BF16)

    bufs = ((sta_sc, mxa_sc), (stb_sc, mxb_sc))
    scores(0, *bufs[0])
    for sub in range(nsub):
        if sub + 1 < nsub:
            scores(sub + 1, *bufs[(sub + 1) % 2])
        update(sub, *bufs[sub % 2])


def _dil(p, tables, bsz, seq):
    m = p.shape[1]
    tq = min(DIL_Q, seq)
    nq = seq // tq
    hg = DIL_HEADS_PER_GROUP
    nkeys = sum(w for _, w in _dil_windows(seq))

    def qspec(g):
        return pl.BlockSpec((1, tq, HEAD_DIM), lambda b, h, i: (C_QC + hg * g + h, b * nq + i, 0))

    def kvspec(base, g):
        return _resident((1, seq, HEAD_DIM), lambda b, h, i: (base + hg * g + h, b, 0))

    def tspec(t):
        return _resident((1,) + t.shape[1:], lambda b, h, i: (h, 0, 0))

    return pl.pallas_call(
        _dil_kernel,
        grid=(bsz, hg, nq),
        in_specs=[qspec(0), qspec(1), qspec(2),
                  kvspec(C_KC, 0), kvspec(C_KC, 1), kvspec(C_KC, 2),
                  kvspec(C_VC, 0), kvspec(C_VC, 1), kvspec(C_VC, 2),
                  tspec(tables[0]), tspec(tables[1]), tspec(tables[2])],
        out_specs=pl.BlockSpec((1, tq, HEAD_DIM), lambda b, h, i: (h, b * nq + i, 0)),
        out_shape=jax.ShapeDtypeStruct((hg, m, HEAD_DIM), BF16),
        scratch_shapes=[pltpu.VMEM((nkeys, DIL_T), F32), pltpu.VMEM((nkeys, DIL_T), F32),
                        pltpu.VMEM((8, DIL_T), F32), pltpu.VMEM((8, DIL_T), F32),
                        pltpu.VMEM((nkeys, DIL_T), BF16)],
        compiler_params=_cparams(("parallel", "parallel", "arbitrary"), 56),
        name="dilated",
    )(p, p, p, p, p, p, p, p, p, *tables)


def _dil_tables(seq):
    slopes = 2.0 ** (-ALIBI_MAX_EXP * jnp.arange(1, DIL_HEADS + 1, dtype=F32) / DIL_HEADS)
    out = []
    for g, (hw, wlen) in enumerate(_dil_windows(seq)):
        win, dil = DIL_GROUPS[g]
        r = np.arange(2 * wlen - DIL_T)[:, None]
        i = np.arange(DIL_T)[None, :]
        rel = i - r + wlen - DIL_T
        ok = jnp.asarray((np.abs(rel) <= win // 2) & (rel % dil == 0))
        dist = jnp.asarray(np.abs(rel), F32)
        sl = slopes[DIL_HEADS_PER_GROUP * g:DIL_HEADS_PER_GROUP * (g + 1)] * LOG2E
        out.append(jnp.where(ok[None], -sl[:, None, None] * dist[None], NEG_INF))
    return out


def _merge_kernel(ya_ref, yb_ref, yc_ref, za0_ref, za1_ref, zb0_ref, zb1_ref, zc_ref, ga_ref, gb_ref, gc_ref,
                  bg_ref, x_ref, wa_ref, wb_ref, wc_ref, wo_ref, pg_ref, o_ref):
    d = x_ref.shape[1]

    def cat(*refs):
        return jnp.concatenate([r[c] for r in refs for c in range(r.shape[0])], axis=-1).astype(F32)

    def branch(y_ref, z_refs, w_ref):
        z = cat(*z_refs)
        u = (cat(y_ref) * (z * jax.nn.sigmoid(z))).astype(BF16)
        return jnp.dot(u, w_ref[...], preferred_element_type=F32)

    def gate(g_ref, k):
        return jax.nn.sigmoid(cat(g_ref) + bg_ref[:, k * d:(k + 1) * d])

    merged = (gate(ga_ref, 0) * branch(ya_ref, (za0_ref, za1_ref), wa_ref)
              + gate(gb_ref, 1) * branch(yb_ref, (zb0_ref, zb1_ref), wb_ref)
              + gate(gc_ref, 2) * branch(yc_ref, (zc_ref,), wc_ref))
    out = jnp.dot(merged.astype(BF16), wo_ref[...], preferred_element_type=F32)
    ms = jnp.mean(out * out, axis=-1, keepdims=True)
    o_ref[...] = x_ref[...] + out * lax.rsqrt(ms + EPS) * pg_ref[...]


def _merge(ya, yb, yc, p, bg, x2d, wa, wb, wc, wo, pg, layer, tm):
    m, d = x2d.shape
    gch = d // V7X_LANES

    zh = NA_HEADS // 2

    def chunks(n, first):
        assert first % n == 0
        return pl.BlockSpec((n, tm, HEAD_DIM), lambda i: (first // n, i, 0))

    def const(shape):
        return _resident(shape, lambda i: (0, 0))

    def weight(w):
        return _resident((None,) + w.shape[1:], lambda i: (layer, 0, 0))

    return pl.pallas_call(
        _merge_kernel,
        grid=(m // tm,),
        in_specs=[
            chunks(NA_HEADS, 0), chunks(GQ_HEADS, 0), chunks(DIL_HEADS_PER_GROUP, 0),
            chunks(zh, C_ZA), chunks(zh, C_ZA + zh), chunks(zh, C_ZB), chunks(zh, C_ZB + zh),
            chunks(DIL_HEADS_PER_GROUP, C_ZC),
            chunks(gch, C_GATE), chunks(gch, C_GATE + gch), chunks(gch, C_GATE + 2 * gch),
            const((1, 3 * d)),
            pl.BlockSpec((tm, d), lambda i: (i, 0)),
            weight(wa), weight(wb), weight(wc), weight(wo),
            const((1, d)),
        ],
        out_specs=pl.BlockSpec((tm, d), lambda i: (i, 0)),
        out_shape=jax.ShapeDtypeStruct((m, d), F32),
        compiler_params=_cparams(("parallel",), 56),
        name="merge",
    )(ya, yb, yc, p, p, p, p, p, p, p, p, bg.reshape(1, 3 * d), x2d, wa, wb, wc, wo, pg.reshape(1, d))


def _layer(x2d, layer, bsz, seq, pre_g, w_in, b_gate, q_g, k_g, na_bias, wa, wb, wc, wo, post_g, cos, sin,
           tables):
    p = _inproj(x2d, pre_g, w_in, layer, tm=min(1024, x2d.shape[0]))
    nprep = (GQ_HEADS + GKV_HEADS) // 2
    gains = jnp.concatenate([jnp.broadcast_to(q_g * QSCALE, (GQ_HEADS // 2, HEAD_DIM)),
                             jnp.broadcast_to(k_g, (GKV_HEADS // 2, HEAD_DIM))], axis=0)
    qk = _qkprep(p, gains.reshape(nprep, 1, HEAD_DIM), cos, sin, seq, tr=min(1024, seq))
    ya = _na(p, na_bias, layer, bsz, seq)
    yb = _gqa(qk, p, bsz, seq, tq=min(256, seq), tk=min(512, seq // GQA_UNROLL),
              ntiles=min(GQA_TILES, seq // min(256, seq)))
    yc = _dil(p, tables, bsz, seq)
    return _merge(ya, yb, yc, p, b_gate, x2d, wa, wb, wc, wo, post_g, layer, tm=256)


def kernel(x, pre_norm_g, w_in, b_gate, q_norm_g, k_norm_g, rpb, w_branch_a, w_branch_b, w_branch_c, w_out,
           post_norm_g):
    bsz, seq, d = x.shape
    depth = w_in.shape[0]
    cos, sin = _rope_tables(seq)
    tables = _dil_tables(seq)
    na_bias = _na_bias(rpb, seq)
    w_in, wa, wb, wc, wo = (w.astype(BF16) for w in (w_in, w_branch_a, w_branch_b, w_branch_c, w_out))
    x2d = x.reshape(bsz * seq, d)
    for l in range(depth):
        x2d = _layer(x2d, l, bsz, seq, pre_norm_g[l], w_in, b_gate[l], q_norm_g[l], k_norm_g[l], na_bias,
                     wa, wb, wc, wo, post_norm_g[l], cos, sin, tables)
    return x2d.reshape(bsz, seq, d)
```

```python
import functools
import math

import numpy as np
import jax
import jax.numpy as jnp
from jax import lax
from jax.experimental import pallas as pl
from jax.experimental.pallas import tpu as pltpu

F32 = jnp.float32
BF16 = jnp.bfloat16

HEAD_DIM = 128
GRID_W = 64
EPS = 1e-6
NEG_INF = -1e30
NA_HEADS = 8
NA_WIN_R = 8
NA_WIN_C = 16
GQ_HEADS = 8
GKV_HEADS = 2
ROPE_BASE = 10000.0
DIL_GROUPS = ((128, 1), (512, 4), (2048, 16))
DIL_HEADS_PER_GROUP = 4
DIL_HEADS = 12
ALIBI_MAX_EXP = 8.0
LOG2E = math.log2(math.e)
QSCALE = HEAD_DIM ** -0.5 * LOG2E

V7X_LANES = 128
MIB = 1024 * 1024

CHUNKS_PER_TILE = 20
SUB_CHUNKS = 4
N_CHUNKS = 140
N_TILES = N_CHUNKS // CHUNKS_PER_TILE
TILE_ROT = 3
C_VC = 0
C_ZA, C_ZB, C_ZC = 12, 20, 28
C_GATE = 32
C_QA, C_KA, C_VA = 80, 88, 96
C_QB, C_KB, C_VB = 104, 112, 114
C_QC, C_KC = 116, 128

NA_R = 4
NA_KW = NA_R + NA_WIN_R - 1
GQA_UNROLL = 4
GQA_TILES = 4
DIL_T = 256
DIL_Q = 8 * DIL_T
DIL_ROWS = 256
DIL_ALIGN = 64


def _cparams(sem, vmem_mib):
    return pltpu.CompilerParams(dimension_semantics=sem, vmem_limit_bytes=int(vmem_mib * MIB))


def _resident(shape, index_map):
    return pl.BlockSpec(shape, index_map, pipeline_mode=pl.Buffered(1))


def _inproj_kernel(x_ref, g_ref, w_ref, p_ref, xn_ref):
    @pl.when(pl.program_id(1) == 0)
    def _():
        x = x_ref[...]
        ms = jnp.mean(x * x, axis=-1, keepdims=True)
        xn_ref[...] = (x * lax.rsqrt(ms + EPS) * g_ref[...]).astype(BF16)

    sub_w = SUB_CHUNKS * V7X_LANES
    for s in range(CHUNKS_PER_TILE // SUB_CHUNKS):
        res = jnp.dot(xn_ref[...], w_ref[:, s * sub_w:(s + 1) * sub_w], preferred_element_type=F32)
        for c in range(SUB_CHUNKS):
            p_ref[s * SUB_CHUNKS + c] = res[:, c * V7X_LANES:(c + 1) * V7X_LANES].astype(BF16)


def _inproj(x2d, g, w, layer, tm):
    m, d = x2d.shape
    tn = CHUNKS_PER_TILE * V7X_LANES
    return pl.pallas_call(
        _inproj_kernel,
        grid=(m // tm, N_TILES),
        in_specs=[
            pl.BlockSpec((tm, d), lambda i, j: (i, 0)),
            pl.BlockSpec((1, d), lambda i, j: (0, 0)),
            pl.BlockSpec((None, d, tn), lambda i, j: (layer, 0, (j + TILE_ROT) % N_TILES)),
        ],
        out_specs=pl.BlockSpec((CHUNKS_PER_TILE, tm, V7X_LANES), lambda i, j: (j, i, 0)),
        out_shape=jax.ShapeDtypeStruct((N_CHUNKS, m, V7X_LANES), BF16),
        scratch_shapes=[pltpu.VMEM((tm, d), BF16)],
        compiler_params=_cparams(("parallel", "arbitrary"), 58),
        name="inproj",
    )(x2d, g.reshape(1, d), w)


def _qkprep_kernel(p_ref, gn_ref, cos_ref, sin_ref, perm_ref, o_ref):
    nc, tr, e = p_ref.shape
    cos = cos_ref[...]
    sin = sin_ref[...]
    gn = gn_ref[0]
    perm = perm_ref[...]
    for c in range(nc):
        x = p_ref[c].astype(F32)
        ms = jnp.mean(x * x, axis=-1, keepdims=True)
        y = x * lax.rsqrt(ms + EPS) * gn
        y_hi = y.astype(BF16)
        y_lo = (y - y_hi.astype(F32)).astype(BF16)
        partner = (jnp.dot(y_hi, perm, preferred_element_type=F32)
                   + jnp.dot(y_lo, perm, preferred_element_type=F32))
        o_ref[c] = (y * cos + partner * sin).astype(BF16)


def _rope_partner_matrix():
    quarter = HEAD_DIM // 4
    j = np.arange(HEAD_DIM)
    src = np.where((j % (2 * quarter)) < quarter, j + quarter, j - quarter)
    perm = np.zeros((HEAD_DIM, HEAD_DIM), np.float32)
    perm[src, j] = 1.0
    return jnp.asarray(perm, BF16)


def _qkprep(p, gains, cos, sin, seq, tr):
    m = p.shape[1]
    nseq = seq // tr
    nblk = (GQ_HEADS + GKV_HEADS) // 2
    return pl.pallas_call(
        _qkprep_kernel,
        grid=(nblk, m // tr),
        in_specs=[
            pl.BlockSpec((2, tr, HEAD_DIM), lambda c, i: (C_QB // 2 + c, i, 0)),
            pl.BlockSpec((1, 1, HEAD_DIM), lambda c, i: (c, 0, 0)),
            pl.BlockSpec((tr, HEAD_DIM), lambda c, i: (i % nseq, 0)),
            pl.BlockSpec((tr, HEAD_DIM), lambda c, i: (i % nseq, 0)),
            pl.BlockSpec((HEAD_DIM, HEAD_DIM), lambda c, i: (0, 0)),
        ],
        out_specs=pl.BlockSpec((2, tr, HEAD_DIM), lambda c, i: (c, i, 0)),
        out_shape=jax.ShapeDtypeStruct((GQ_HEADS + GKV_HEADS, m, HEAD_DIM), BF16),
        compiler_params=_cparams(("parallel", "parallel"), 32),
        name="qkprep",
    )(p, gains, cos, sin, _rope_partner_matrix())


def _rope_tables(seq):
    quarter = HEAD_DIM // 4
    freqs = ROPE_BASE ** (-jnp.arange(quarter, dtype=F32) / quarter)
    t = jnp.arange(seq)
    ang_r = (t // GRID_W).astype(F32)[:, None] * freqs[None, :]
    ang_c = (t % GRID_W).astype(F32)[:, None] * freqs[None, :]
    cos = jnp.concatenate([jnp.cos(ang_r), jnp.cos(ang_r), jnp.cos(ang_c), jnp.cos(ang_c)], axis=-1)
    sin = jnp.concatenate([-jnp.sin(ang_r), jnp.sin(ang_r), -jnp.sin(ang_c), jnp.sin(ang_c)], axis=-1)
    return cos, sin


def _gqa_kernel(q_ref, k_ref, v_ref, o_ref, qt_sc, st0_sc, st1_sc, mx0_sc, mx1_sc, m_sc, l_sc, acc_sc, *, tq, tk):
    grp, tq_step, e = q_ref.shape
    seq = k_ref.shape[1]
    ntiles = tq_step // tq
    nq = grp * tq
    nchunks = seq // tk
    for t in range(ntiles):
        qt_sc[t] = q_ref[:, pl.ds(t * tq, tq), :].reshape(nq, e).T

    def scores(t, i, bufs):
        dst, mx_dst = bufs
        start = pl.multiple_of(i * tk, tk)
        st = jnp.dot(k_ref[0, pl.ds(start, tk), :], qt_sc[t], preferred_element_type=F32)
        dst[...] = st
        mx_dst[...] = jnp.max(st.reshape(tk // 8, 8, nq), axis=0)

    def update(i, bufs):
        src, mx_src = bufs
        start = pl.multiple_of(i * tk, tk)
        vt = v_ref[0, pl.ds(start, tk), :].T
        st = src[...]
        m_old = m_sc[...]
        m_new = jnp.maximum(m_old, jnp.max(mx_src[...], axis=0, keepdims=True))
        alpha = jnp.exp2(m_old - m_new)
        pt = jnp.exp2(st - m_new)
        l_sc[...] = alpha * l_sc[...] + jnp.sum(pt, axis=0, keepdims=True)
        acc_sc[...] = alpha * acc_sc[...] + jnp.dot(vt, pt.astype(BF16), preferred_element_type=F32)
        m_sc[...] = m_new

    bufs = ((st0_sc, mx0_sc), (st1_sc, mx1_sc))
    scores(0, 0, bufs[0])
    for t in range(ntiles):
        m_sc[...] = jnp.full(m_sc.shape, -jnp.inf, F32)
        l_sc[...] = jnp.zeros(l_sc.shape, F32)
        acc_sc[...] = jnp.zeros(acc_sc.shape, F32)

        def quad(j, carry, t=t):
            for u in range(GQA_UNROLL):
                scores(t, GQA_UNROLL * j + u + 1, bufs[(u + 1) % 2])
                update(GQA_UNROLL * j + u, bufs[u % 2])
            return carry

        lax.fori_loop(0, nchunks // GQA_UNROLL - 1, quad, 0)
        base = nchunks - GQA_UNROLL
        for u in range(GQA_UNROLL):
            if u + 1 < GQA_UNROLL:
                scores(t, base + u + 1, bufs[(u + 1) % 2])
            elif t + 1 < ntiles:
                scores(t + 1, 0, bufs[(u + 1) % 2])
            update(base + u, bufs[u % 2])
        o = (acc_sc[...] / l_sc[...]).T
        o_ref[:, pl.ds(t * tq, tq), :] = o.astype(BF16).reshape(grp, tq, e)


def _gqa(qk, p, bsz, seq, tq, tk, ntiles):
    m = qk.shape[1]
    grp = GQ_HEADS // GKV_HEADS
    tq_step = tq * ntiles
    nq = seq // tq_step
    return pl.pallas_call(
        functools.partial(_gqa_kernel, tq=tq, tk=tk),
        grid=(bsz, GKV_HEADS, nq),
        in_specs=[
            pl.BlockSpec((grp, tq_step, HEAD_DIM), lambda b, g, i: (g, b * nq + i, 0)),
            pl.BlockSpec((1, seq, HEAD_DIM), lambda b, g, i: (GQ_HEADS + g, b, 0)),
            pl.BlockSpec((1, seq, HEAD_DIM), lambda b, g, i: (C_VB + g, b, 0)),
        ],
        out_specs=pl.BlockSpec((grp, tq_step, HEAD_DIM), lambda b, g, i: (g, b * nq + i, 0)),
        out_shape=jax.ShapeDtypeStruct((GQ_HEADS, m, HEAD_DIM), BF16),
        scratch_shapes=[pltpu.VMEM((ntiles, HEAD_DIM, grp * tq), BF16),
                        pltpu.VMEM((tk, grp * tq), F32), pltpu.VMEM((tk, grp * tq), F32),
                        pltpu.VMEM((8, grp * tq), F32), pltpu.VMEM((8, grp * tq), F32),
                        pltpu.VMEM((1, grp * tq), F32), pltpu.VMEM((1, grp * tq), F32),
                        pltpu.VMEM((HEAD_DIM, grp * tq), F32)],
        compiler_params=_cparams(("parallel", "parallel", "arbitrary"), 48),
        name="gqa",
    )(qk, qk, p)


def _na_kernel(q_ref, k_ref, v_ref, bias_ref, o_ref, st_sc, mx_sc):
    seq = q_ref.shape[1]
    nq = NA_R * GRID_W
    nkeys = NA_KW * GRID_W
    nblk = seq // nq
    bufs = [(st_sc.at[i], mx_sc.at[i]) for i in range(4)]

    def window(b):
        lead = (NA_WIN_R // 2) * GRID_W
        if isinstance(b, int):
            if b == 0:
                return 0, 0, 0
            if b == nblk - 1:
                return seq - nq, seq - nkeys, 2
            return b * nq, b * nq - lead, 1
        q0 = pl.multiple_of(b * nq, nq)
        return q0, pl.multiple_of(q0 - lead, GRID_W), 1

    def scores(b, bufs):
        dst, mx_dst = bufs
        q0, k0, variant = window(b)
        qt = (q_ref[0, pl.ds(q0, nq), :].astype(F32) * QSCALE).astype(BF16).T
        k = k_ref[0, pl.ds(k0, nkeys), :]
        st = jnp.dot(k, qt, preferred_element_type=F32) + bias_ref[variant, 0]
        dst[...] = st
        mx_dst[...] = jnp.max(st.reshape(nkeys // 8, 8, nq), axis=0)

    def update(b, bufs):
        src, mx_src = bufs
        q0, k0, _ = window(b)
        st = src[...]
        vt = v_ref[0, pl.ds(k0, nkeys), :].T
        pt = jnp.exp2(st - jnp.max(mx_src[...], axis=0, keepdims=True))
        l = jnp.sum(pt, axis=0, keepdims=True)
        ot = jnp.dot(vt, pt.astype(BF16), preferred_element_type=F32)
        o_ref[0, pl.ds(q0, nq), :] = (ot / l).T.astype(BF16)

    scores(0, bufs[0])
    scores(1, bufs[1])

    def quad(j, carry=0):
        for u in (0, 1):
            b = 4 * j + 2 * u
            scores(b + 2, bufs[(2 * u + 2) % 4])
            scores(b + 3, bufs[(2 * u + 3) % 4])
            update(b, bufs[2 * u])
            update(b + 1, bufs[2 * u + 1])
        return carry

    if nblk > 4:
        quad(0)
        lax.fori_loop(1, nblk // 4 - 1, quad, 0)
    base = nblk - 4
    scores(base + 2, bufs[2])
    scores(base + 3, bufs[3])
    update(base, bufs[0])
    update(base + 1, bufs[1])
    update(base + 2, bufs[2])
    update(base + 3, bufs[3])


def _na(p, bias, layer, bsz, seq):
    m = p.shape[1]
    nq = NA_R * GRID_W
    nkeys = NA_KW * GRID_W
    return pl.pallas_call(
        _na_kernel,
        grid=(NA_HEADS, bsz),
        in_specs=[
            pl.BlockSpec((1, seq, HEAD_DIM), lambda h, b: (C_QA + h, b, 0)),
            pl.BlockSpec((1, seq, HEAD_DIM), lambda h, b: (C_KA + h, b, 0)),
            pl.BlockSpec((1, seq, HEAD_DIM), lambda h, b: (C_VA + h, b, 0)),
            pl.BlockSpec((None, 3, 1, nkeys, nq), lambda h, b: (layer, 0, h, 0, 0)),
        ],
        out_specs=pl.BlockSpec((1, seq, HEAD_DIM), lambda h, b: (h, b, 0)),
        out_shape=jax.ShapeDtypeStruct((NA_HEADS, m, HEAD_DIM), BF16),
        scratch_shapes=[pltpu.VMEM((4, nkeys, nq), F32), pltpu.VMEM((4, 8, nq), F32)],
        compiler_params=_cparams(("parallel", "parallel"), 48),
        name="na",
    )(p, p, p, bias)


def _na_bias(rpb, seq):
    rows = seq // GRID_W
    kr = min(NA_WIN_R, rows)
    ndr = 2 * NA_WIN_R - 1
    ndc = 2 * NA_WIN_C - 1
    c = np.arange(GRID_W)[:, None]
    j = np.arange(GRID_W)[None, :]
    cs = np.clip(j - NA_WIN_C // 2, 0, GRID_W - NA_WIN_C)
    col_ok = (c >= cs) & (c < cs + NA_WIN_C)
    dc = np.clip(c - j + NA_WIN_C - 1, 0, ndc - 1)
    onehot = (dc[:, :, None] == np.arange(ndc)[None, None, :]).astype(np.float32)
    tiles = jnp.einsum('lhrd,cjd->lhrcj', rpb.astype(F32), jnp.asarray(onehot),
                       precision=lax.Precision.HIGHEST)
    tiles = jnp.where(jnp.asarray(col_ok), tiles * LOG2E, NEG_INF)
    masked = jnp.full(tiles.shape[:2] + (1, GRID_W, GRID_W), NEG_INF, F32)
    tiles = jnp.concatenate([tiles, masked], axis=2)
    index = []
    for r0, k0 in ((0, 0), (NA_R, NA_R - kr // 2), (rows - NA_R, rows - NA_KW)):
        r = r0 + np.arange(NA_R)[None, :]
        krow = k0 + np.arange(NA_KW)[:, None]
        start = np.clip(r - kr // 2, 0, rows - kr)
        row_ok = (krow >= start) & (krow < start + kr)
        index.append(np.where(row_ok, krow - r + NA_WIN_R - 1, ndr))
    index = np.stack(index).reshape(-1)
    sel = jnp.take(tiles, jnp.asarray(index), axis=2)
    sel = sel.reshape(sel.shape[:2] + (3, NA_KW, NA_R, GRID_W, GRID_W))
    sel = sel.transpose(0, 2, 1, 3, 5, 4, 6)
    return sel.reshape(sel.shape[:3] + (NA_KW * GRID_W, NA_R * GRID_W))


def _dil_windows(seq):
    out = []
    for win, _ in DIL_GROUPS:
        hw = -(-(win // 2) // DIL_ALIGN) * DIL_ALIGN
        out.append((hw, min(2 * hw + DIL_T, seq)))
    return out


def _dil_kernel(q0_ref, q1_ref, q2_ref, k0_ref, k1_ref, k2_ref, v0_ref, v1_ref, v2_ref,
                e0_ref, e1_ref, e2_ref, o_ref, st_sc, mx_sc, pt_sc):
    q_refs = (q0_ref, q1_ref, q2_ref)
    k_refs = (k0_ref, k1_ref, k2_ref)
    v_refs = (v0_ref, v1_ref, v2_ref)
    e_refs = (e0_ref, e1_ref, e2_ref)
    seq = k0_ref.shape[1]
    nsub = q0_ref.shape[1] // DIL_T
    windows = _dil_windows(seq)
    offs = [sum(w for _, w in windows[:g]) for g in range(len(windows))]

    def key_start(sub, g):
        t0 = pl.program_id(2) * q0_ref.shape[1] + sub * DIL_T
        hw, wlen = windows[g]
        ws = pl.multiple_of(jnp.clip(t0 - hw, 0, seq - wlen), DIL_ALIGN)
        return ws, pl.multiple_of(wlen - DIL_T - (t0 - ws), DIL_ALIGN)

    def scores(sub, dst, mx_dst):
        m8 = None
        for g, (hw, wlen) in enumerate(windows):
            ws, erow = key_start(sub, g)
            qt = (q_refs[g][0, pl.ds(sub * DIL_T, DIL_T), :].astype(F32) * QSCALE).astype(BF16).T
            for r in range(0, wlen, DIL_ROWS):
                n = min(DIL_ROWS, wlen - r)
                k = k_refs[g][0, pl.ds(ws + r, n), :]
                st = (jnp.dot(k, qt, preferred_element_type=F32)
                      + e_refs[g][0, pl.ds(erow + r, n), :])
                dst[pl.ds(offs[g] + r, n), :] = st
                m = jnp.max(st.reshape(n // 8, 8, DIL_T), axis=0)
                m8 = m if m8 is None else jnp.maximum(m8, m)
        mx_dst[...] = m8

    def update(sub, src, mx_src, pt_dst):
        mx = jnp.max(mx_src[...], axis=0, keepdims=True)
        nkeys = src.shape[0]
        l8 = jnp.zeros((8, DIL_T), F32)
        for r in range(0, nkeys, DIL_ROWS):
            n = min(DIL_ROWS, nkeys - r)
            pt = jnp.exp2(src[pl.ds(r, n), :] - mx)
            l8 = l8 + jnp.sum(pt.reshape(n // 8, 8, DIL_T), axis=0)
            pt_dst[pl.ds(r, n), :] = pt.astype(BF16)
        l = jnp.sum(l8, axis=0, keepdims=True)
        acc = jnp.zeros((HEAD_DIM, DIL_T), F32)
        for g, (hw, wlen) in enumerate(windows):
            ws, _ = key_start(sub, g)
            vt = v_refs[g][0, pl.ds(ws, wlen), :].T
            acc = acc + jnp.dot(vt, pt_dst[pl.ds(offs[g], wlen), :], preferred_element_type=F32)
        o_ref[0, pl.ds(sub * DIL_T, DIL_T), :] = (acc / l).T.astype(BF16)

    bufs = [(st_sc.at[i], mx_sc.at[i]) for i in range(4)]
    scores(0, *bufs[0])
    scores(1, *bufs[1])
    for sub in range(0, nsub, 2):
        if sub + 2 < nsub:
            scores(sub + 2, *bufs[(sub + 2) % 4])
            scores(sub + 3, *bufs[(sub + 3) % 4])
        update(sub, *bufs[sub % 4], pt_sc.at[0])
        update(sub + 1, *bufs[(sub + 1) % 4], pt_sc.at[1])


def _dil(p, tables, bsz, seq):
    m = p.shape[1]
    tq = min(DIL_Q, seq)
    nq = seq // tq
    hg = DIL_HEADS_PER_GROUP
    nkeys = sum(w for _, w in _dil_windows(seq))

    def qspec(g):
        return pl.BlockSpec((1, tq, HEAD_DIM), lambda b, h, i: (C_QC + hg * g + h, b * nq + i, 0))

    def kvspec(base, g):
        return _resident((1, seq, HEAD_DIM), lambda b, h, i: (base + hg * g + h, b, 0))

    def tspec(t):
        return _resident((1,) + t.shape[1:], lambda b, h, i: (h, 0, 0))

    return pl.pallas_call(
        _dil_kernel,
        grid=(bsz, hg, nq),
        in_specs=[qspec(0), qspec(1), qspec(2),
                  kvspec(C_KC, 0), kvspec(C_KC, 1), kvspec(C_KC, 2),
                  kvspec(C_VC, 0), kvspec(C_VC, 1), kvspec(C_VC, 2),
                  tspec(tables[0]), tspec(tables[1]), tspec(tables[2])],
        out_specs=pl.BlockSpec((1, tq, HEAD_DIM), lambda b, h, i: (h, b * nq + i, 0)),
        out_shape=jax.ShapeDtypeStruct((hg, m, HEAD_DIM), BF16),
        scratch_shapes=[pltpu.VMEM((4, nkeys, DIL_T), F32), pltpu.VMEM((4, 8, DIL_T), F32),
                        pltpu.VMEM((2, nkeys, DIL_T), BF16)],
        compiler_params=_cparams(("parallel", "parallel", "arbitrary"), 56),
        name="dilated",
    )(p, p, p, p, p, p, p, p, p, *tables)


def _dil_tables(seq):
    slopes = 2.0 ** (-ALIBI_MAX_EXP * jnp.arange(1, DIL_HEADS + 1, dtype=F32) / DIL_HEADS)
    out = []
    for g, (hw, wlen) in enumerate(_dil_windows(seq)):
        win, dil = DIL_GROUPS[g]
        r = np.arange(2 * wlen - DIL_T)[:, None]
        i = np.arange(DIL_T)[None, :]
        rel = i - r + wlen - DIL_T
        ok = jnp.asarray((np.abs(rel) <= win // 2) & (rel % dil == 0))
        dist = jnp.asarray(np.abs(rel), F32)
        sl = slopes[DIL_HEADS_PER_GROUP * g:DIL_HEADS_PER_GROUP * (g + 1)] * LOG2E
        out.append(jnp.where(ok[None], -sl[:, None, None] * dist[None], NEG_INF))
    return out


def _merge_kernel(ya_ref, yb_ref, yc_ref, za0_ref, za1_ref, zb0_ref, zb1_ref, zc_ref, ga_ref, gb_ref, gc_ref,
                  bg_ref, x_ref, wa_ref, wb_ref, wc_ref, wo_ref, pg_ref, o_ref):
    d = x_ref.shape[1]

    def cat(*refs):
        return jnp.concatenate([r[c] for r in refs for c in range(r.shape[0])], axis=-1).astype(F32)

    def branch(y_ref, z_refs, w_ref):
        z = cat(*z_refs)
        u = (cat(y_ref) * (z * jax.nn.sigmoid(z))).astype(BF16)
        return jnp.dot(u, w_ref[...], preferred_element_type=F32)

    def gate(g_ref, k):
        return jax.nn.sigmoid(cat(g_ref) + bg_ref[:, k * d:(k + 1) * d])

    merged = (gate(ga_ref, 0) * branch(ya_ref, (za0_ref, za1_ref), wa_ref)
              + gate(gb_ref, 1) * branch(yb_ref, (zb0_ref, zb1_ref), wb_ref)
              + gate(gc_ref, 2) * branch(yc_ref, (zc_ref,), wc_ref))
    out = jnp.dot(merged.astype(BF16), wo_ref[...], preferred_element_type=F32)
    ms = jnp.mean(out * out, axis=-1, keepdims=True)
    o_ref[...] = x_ref[...] + out * lax.rsqrt(ms + EPS) * pg_ref[...]


def _merge(ya, yb, yc, p, bg, x2d, wa, wb, wc, wo, pg, layer, tm):
    m, d = x2d.shape
    gch = d // V7X_LANES

    zh = NA_HEADS // 2

    def chunks(n, first):
        assert first % n == 0
        return pl.BlockSpec((n, tm, HEAD_DIM), lambda i: (first // n, i, 0))

    def const(shape):
        return _resident(shape, lambda i: (0, 0))

    def weight(w):
        return _resident((None,) + w.shape[1:], lambda i: (layer, 0, 0))

    return pl.pallas_call(
        _merge_kernel,
        grid=(m // tm,),
        in_specs=[
            chunks(NA_HEADS, 0), chunks(GQ_HEADS, 0), chunks(DIL_HEADS_PER_GROUP, 0),
            chunks(zh, C_ZA), chunks(zh, C_ZA + zh), chunks(zh, C_ZB), chunks(zh, C_ZB + zh),
            chunks(DIL_HEADS_PER_GROUP, C_ZC),
            chunks(gch, C_GATE), chunks(gch, C_GATE + gch), chunks(gch, C_GATE + 2 * gch),
            const((1, 3 * d)),
            pl.BlockSpec((tm, d), lambda i: (i, 0)),
            weight(wa), weight(wb), weight(wc), weight(wo),
            const((1, d)),
        ],
        out_specs=pl.BlockSpec((tm, d), lambda i: (i, 0)),
        out_shape=jax.ShapeDtypeStruct((m, d), F32),
        compiler_params=_cparams(("parallel",), 56),
        name="merge",
    )(ya, yb, yc, p, p, p, p, p, p, p, p, bg.reshape(1, 3 * d), x2d, wa, wb, wc, wo, pg.reshape(1, d))


def _layer(x2d, layer, bsz, seq, pre_g, w_in, b_gate, q_g, k_g, na_bias, wa, wb, wc, wo, post_g, cos, sin,
           tables):
    p = _inproj(x2d, pre_g, w_in, layer, tm=min(1024, x2d.shape[0]))
    nprep = (GQ_HEADS + GKV_HEADS) // 2
    gains = jnp.concatenate([jnp.broadcast_to(q_g * QSCALE, (GQ_HEADS // 2, HEAD_DIM)),
                             jnp.broadcast_to(k_g, (GKV_HEADS // 2, HEAD_DIM))], axis=0)
    qk = _qkprep(p, gains.reshape(nprep, 1, HEAD_DIM), cos, sin, seq, tr=min(1024, seq))
    ya = _na(p, na_bias, layer, bsz, seq)
    yb = _gqa(qk, p, bsz, seq, tq=min(256, seq), tk=min(512, seq // GQA_UNROLL),
              ntiles=min(GQA_TILES, seq // min(256, seq)))
    yc = _dil(p, tables, bsz, seq)
    return _merge(ya, yb, yc, p, b_gate, x2d, wa, wb, wc, wo, post_g, layer, tm=256)


def kernel(x, pre_norm_g, w_in, b_gate, q_norm_g, k_norm_g, rpb, w_branch_a, w_branch_b, w_branch_c, w_out,
           post_norm_g):
    bsz, seq, d = x.shape
    depth = w_in.shape[0]
    cos, sin = _rope_tables(seq)
    tables = _dil_tables(seq)
    na_bias = _na_bias(rpb, seq)
    w_in, wa, wb, wc, wo = (w.astype(BF16) for w in (w_in, w_branch_a, w_branch_b, w_branch_c, w_out))
    x2d = x.reshape(bsz * seq, d)
    for l in range(depth):
        x2d = _layer(x2d, l, bsz, seq, pre_norm_g[l], w_in, b_gate[l], q_norm_g[l], k_norm_g[l], na_bias,
                     wa, wb, wc, wo, post_norm_g[l], cos, sin, tables)
    return x2d.reshape(bsz, seq, d)
```

```python
import functools
import math

import numpy as np
import jax
import jax.numpy as jnp
from jax import lax
from jax.experimental import pallas as pl
from jax.experimental.pallas import tpu as pltpu

F32 = jnp.float32
BF16 = jnp.bfloat16

HEAD_DIM = 128
GRID_W = 64
EPS = 1e-6
NEG_INF = -1e30
NA_HEADS = 8
NA_WIN_R = 8
NA_WIN_C = 16
GQ_HEADS = 8
GKV_HEADS = 2
ROPE_BASE = 10000.0
DIL_GROUPS = ((128, 1), (512, 4), (2048, 16))
DIL_HEADS_PER_GROUP = 4
DIL_HEADS = 12
ALIBI_MAX_EXP = 8.0
LOG2E = math.log2(math.e)
QSCALE = HEAD_DIM ** -0.5 * LOG2E

V7X_LANES = 128
MIB = 1024 * 1024

CHUNKS_PER_TILE = 20
SUB_CHUNKS = 4
N_CHUNKS = 140
N_TILES = N_CHUNKS // CHUNKS_PER_TILE
TILE_ROT = 3
C_VC = 0
C_ZA, C_ZB, C_ZC = 12, 20, 28
C_GATE = 32
C_QA, C_KA, C_VA = 80, 88, 96
C_QB, C_KB, C_VB = 104, 112, 114
C_QC, C_KC = 116, 128

NA_R = 4
NA_KW = NA_R + NA_WIN_R - 1
GQA_UNROLL = 4
GQA_TILES = 4
DIL_T = 256
DIL_Q = 8 * DIL_T
DIL_ROWS = 256
DIL_ALIGN = 64


def _cparams(sem, vmem_mib):
    return pltpu.CompilerParams(dimension_semantics=sem, vmem_limit_bytes=int(vmem_mib * MIB))


def _resident(shape, index_map):
    return pl.BlockSpec(shape, index_map, pipeline_mode=pl.Buffered(1))


def _inproj_kernel(x_ref, g_ref, w_ref, p_ref, xn_ref):
    @pl.when(pl.program_id(1) == 0)
    def _():
        x = x_ref[...]
        ms = jnp.mean(x * x, axis=-1, keepdims=True)
        xn_ref[...] = (x * lax.rsqrt(ms + EPS) * g_ref[...]).astype(BF16)

    sub_w = SUB_CHUNKS * V7X_LANES
    for s in range(CHUNKS_PER_TILE // SUB_CHUNKS):
        res = jnp.dot(xn_ref[...], w_ref[:, s * sub_w:(s + 1) * sub_w], preferred_element_type=F32)
        for c in range(SUB_CHUNKS):
            p_ref[s * SUB_CHUNKS + c] = res[:, c * V7X_LANES:(c + 1) * V7X_LANES].astype(BF16)


def _inproj(x2d, g, w, layer, tm):
    m, d = x2d.shape
    tn = CHUNKS_PER_TILE * V7X_LANES
    return pl.pallas_call(
        _inproj_kernel,
        grid=(m // tm, N_TILES),
        in_specs=[
            pl.BlockSpec((tm, d), lambda i, j: (i, 0)),
            pl.BlockSpec((1, d), lambda i, j: (0, 0)),
            pl.BlockSpec((None, d, tn), lambda i, j: (layer, 0, (j + TILE_ROT) % N_TILES)),
        ],
        out_specs=pl.BlockSpec((CHUNKS_PER_TILE, tm, V7X_LANES), lambda i, j: (j, i, 0)),
        out_shape=jax.ShapeDtypeStruct((N_CHUNKS, m, V7X_LANES), BF16),
        scratch_shapes=[pltpu.VMEM((tm, d), BF16)],
        compiler_params=_cparams(("parallel", "arbitrary"), 58),
        name="inproj",
    )(x2d, g.reshape(1, d), w)


def _qkprep_kernel(p_ref, gn_ref, cos_ref, sin_ref, perm_ref, o_ref):
    nc, tr, e = p_ref.shape
    cos = cos_ref[...]
    sin = sin_ref[...]
    gn = gn_ref[0]
    perm = perm_ref[...]
    for c in range(nc):
        x = p_ref[c].astype(F32)
        ms = jnp.mean(x * x, axis=-1, keepdims=True)
        y = x * lax.rsqrt(ms + EPS) * gn
        y_hi = y.astype(BF16)
        y_lo = (y - y_hi.astype(F32)).astype(BF16)
        partner = (jnp.dot(y_hi, perm, preferred_element_type=F32)
                   + jnp.dot(y_lo, perm, preferred_element_type=F32))
        o_ref[c] = (y * cos + partner * sin).astype(BF16)


def _rope_partner_matrix():
    quarter = HEAD_DIM // 4
    j = np.arange(HEAD_DIM)
    src = np.where((j % (2 * quarter)) < quarter, j + quarter, j - quarter)
    perm = np.zeros((HEAD_DIM, HEAD_DIM), np.float32)
    perm[src, j] = 1.0
    return jnp.asarray(perm, BF16)


def _qkprep(p, gains, cos, sin, seq, tr):
    m = p.shape[1]
    nseq = seq // tr
    nblk = (GQ_HEADS + GKV_HEADS) // 2
    return pl.pallas_call(
        _qkprep_kernel,
        grid=(nblk, m // tr),
        in_specs=[
            pl.BlockSpec((2, tr, HEAD_DIM), lambda c, i: (C_QB // 2 + c, i, 0)),
            pl.BlockSpec((1, 1, HEAD_DIM), lambda c, i: (c, 0, 0)),
            pl.BlockSpec((tr, HEAD_DIM), lambda c, i: (i % nseq, 0)),
            pl.BlockSpec((tr, HEAD_DIM), lambda c, i: (i % nseq, 0)),
            pl.BlockSpec((HEAD_DIM, HEAD_DIM), lambda c, i: (0, 0)),
        ],
        out_specs=pl.BlockSpec((2, tr, HEAD_DIM), lambda c, i: (c, i, 0)),
        out_shape=jax.ShapeDtypeStruct((GQ_HEADS + GKV_HEADS, m, HEAD_DIM), BF16),
        compiler_params=_cparams(("parallel", "parallel"), 32),
        name="qkprep",
    )(p, gains, cos, sin, _rope_partner_matrix())


def _rope_tables(seq):
    quarter = HEAD_DIM // 4
    freqs = ROPE_BASE ** (-jnp.arange(quarter, dtype=F32) / quarter)
    t = jnp.arange(seq)
    ang_r = (t // GRID_W).astype(F32)[:, None] * freqs[None, :]
    ang_c = (t % GRID_W).astype(F32)[:, None] * freqs[None, :]
    cos = jnp.concatenate([jnp.cos(ang_r), jnp.cos(ang_r), jnp.cos(ang_c), jnp.cos(ang_c)], axis=-1)
    sin = jnp.concatenate([-jnp.sin(ang_r), jnp.sin(ang_r), -jnp.sin(ang_c), jnp.sin(ang_c)], axis=-1)
    return cos, sin


def _gqa_kernel(q_ref, k_ref, v_ref, o_ref, qt_sc, st0_sc, st1_sc, mx0_sc, mx1_sc, m_sc, l_sc, acc_sc, *, tq, tk):
    grp, tq_step, e = q_ref.shape
    seq = k_ref.shape[1]
    ntiles = tq_step // tq
    nq = grp * tq
    nchunks = seq // tk
    for t in range(ntiles):
        qt_sc[t] = q_ref[:, pl.ds(t * tq, tq), :].reshape(nq, e).T

    def scores(t, i, bufs):
        dst, mx_dst = bufs
        start = pl.multiple_of(i * tk, tk)
        st = jnp.dot(k_ref[0, pl.ds(start, tk), :], qt_sc[t], preferred_element_type=F32)
        dst[...] = st
        mx_dst[...] = jnp.max(st.reshape(tk // 8, 8, nq), axis=0)

    def update(i, bufs):
        src, mx_src = bufs
        start = pl.multiple_of(i * tk, tk)
        vt = v_ref[0, pl.ds(start, tk), :].T
        st = src[...]
        m_old = m_sc[...]
        m_new = jnp.maximum(m_old, jnp.max(mx_src[...], axis=0, keepdims=True))
        alpha = jnp.exp2(m_old - m_new)
        pt = jnp.exp2(st - m_new)
        l_sc[...] = alpha * l_sc[...] + jnp.sum(pt, axis=0, keepdims=True)
        acc_sc[...] = alpha * acc_sc[...] + jnp.dot(vt, pt.astype(BF16), preferred_element_type=F32)
        m_sc[...] = m_new

    bufs = ((st0_sc, mx0_sc), (st1_sc, mx1_sc))
    scores(0, 0, bufs[0])
    for t in range(ntiles):
        m_sc[...] = jnp.full(m_sc.shape, -jnp.inf, F32)
        l_sc[...] = jnp.zeros(l_sc.shape, F32)
        acc_sc[...] = jnp.zeros(acc_sc.shape, F32)

        def quad(j, carry, t=t):
            for u in range(GQA_UNROLL):
                scores(t, GQA_UNROLL * j + u + 1, bufs[(u + 1) % 2])
                update(GQA_UNROLL * j + u, bufs[u % 2])
            return carry

        lax.fori_loop(0, nchunks // GQA_UNROLL - 1, quad, 0)
        base = nchunks - GQA_UNROLL
        for u in range(GQA_UNROLL):
            if u + 1 < GQA_UNROLL:
                scores(t, base + u + 1, bufs[(u + 1) % 2])
            elif t + 1 < ntiles:
                scores(t + 1, 0, bufs[(u + 1) % 2])
            update(base + u, bufs[u % 2])
        o = (acc_sc[...] / l_sc[...]).T
        o_ref[:, pl.ds(t * tq, tq), :] = o.astype(BF16).reshape(grp, tq, e)


def _gqa(qk, p, bsz, seq, tq, tk, ntiles):
    m = qk.shape[1]
    grp = GQ_HEADS // GKV_HEADS
    tq_step = tq * ntiles
    nq = seq // tq_step
    return pl.pallas_call(
        functools.partial(_gqa_kernel, tq=tq, tk=tk),
        grid=(bsz, GKV_HEADS, nq),
        in_specs=[
            pl.BlockSpec((grp, tq_step, HEAD_DIM), lambda b, g, i: (g, b * nq + i, 0)),
            pl.BlockSpec((1, seq, HEAD_DIM), lambda b, g, i: (GQ_HEADS + g, b, 0)),
            pl.BlockSpec((1, seq, HEAD_DIM), lambda b, g, i: (C_VB + g, b, 0)),
        ],
        out_specs=pl.BlockSpec((grp, tq_step, HEAD_DIM), lambda b, g, i: (g, b * nq + i, 0)),
        out_shape=jax.ShapeDtypeStruct((GQ_HEADS, m, HEAD_DIM), BF16),
        scratch_shapes=[pltpu.VMEM((ntiles, HEAD_DIM, grp * tq), BF16),
                        pltpu.VMEM((tk, grp * tq), F32), pltpu.VMEM((tk, grp * tq), F32),
                        pltpu.VMEM((8, grp * tq), F32), pltpu.VMEM((8, grp * tq), F32),
                        pltpu.VMEM((1, grp * tq), F32), pltpu.VMEM((1, grp * tq), F32),
                        pltpu.VMEM((HEAD_DIM, grp * tq), F32)],
        compiler_params=_cparams(("parallel", "parallel", "arbitrary"), 48),
        name="gqa",
    )(qk, qk, p)


def _na_kernel(q_ref, k_ref, v_ref, bias_ref, o_ref, st_sc, mx_sc):
    seq = q_ref.shape[1]
    nq = NA_R * GRID_W
    nkeys = NA_KW * GRID_W
    nblk = seq // nq
    bufs = [(st_sc.at[i], mx_sc.at[i]) for i in range(4)]

    def window(b):
        lead = (NA_WIN_R // 2) * GRID_W
        if isinstance(b, int):
            if b == 0:
                return 0, 0, 0
            if b == nblk - 1:
                return seq - nq, seq - nkeys, 2
            return b * nq, b * nq - lead, 1
        q0 = pl.multiple_of(b * nq, nq)
        return q0, pl.multiple_of(q0 - lead, GRID_W), 1

    def scores(b, bufs):
        dst, mx_dst = bufs
        q0, k0, variant = window(b)
        qt = (q_ref[0, pl.ds(q0, nq), :].astype(F32) * QSCALE).astype(BF16).T
        k = k_ref[0, pl.ds(k0, nkeys), :]
        st = jnp.dot(k, qt, preferred_element_type=F32) + bias_ref[variant, 0]
        dst[...] = st
        mx_dst[...] = jnp.max(st.reshape(nkeys // 8, 8, nq), axis=0)

    def update(b, bufs):
        src, mx_src = bufs
        q0, k0, _ = window(b)
        st = src[...]
        vt = v_ref[0, pl.ds(k0, nkeys), :].T
        pt = jnp.exp2(st - jnp.max(mx_src[...], axis=0, keepdims=True))
        l = jnp.sum(pt, axis=0, keepdims=True)
        ot = jnp.dot(vt, pt.astype(BF16), preferred_element_type=F32)
        o_ref[0, pl.ds(q0, nq), :] = (ot / l).T.astype(BF16)

    scores(0, bufs[0])
    scores(1, bufs[1])

    def quad(j, carry=0):
        for u in (0, 1):
            b = 4 * j + 2 * u
            scores(b + 2, bufs[(2 * u + 2) % 4])
            scores(b + 3, bufs[(2 * u + 3) % 4])
            update(b, bufs[2 * u])
            update(b + 1, bufs[2 * u + 1])
        return carry

    if nblk > 4:
        quad(0)
        lax.fori_loop(1, nblk // 4 - 1, quad, 0)
    base = nblk - 4
    scores(base + 2, bufs[2])
    scores(base + 3, bufs[3])
    update(base, bufs[0])
    update(base + 1, bufs[1])
    update(base + 2, bufs[2])
    update(base + 3, bufs[3])


def _na(p, bias, layer, bsz, seq):
    m = p.shape[1]
    nq = NA_R * GRID_W
    nkeys = NA_KW * GRID_W
    return pl.pallas_call(
        _na_kernel,
        grid=(NA_HEADS, bsz),
        in_specs=[
            pl.BlockSpec((1, seq, HEAD_DIM), lambda h, b: (C_QA + h, b, 0)),
            pl.BlockSpec((1, seq, HEAD_DIM), lambda h, b: (C_KA + h, b, 0)),
            pl.BlockSpec((1, seq, HEAD_DIM), lambda h, b: (C_VA + h, b, 0)),
            pl.BlockSpec((None, 3, 1, nkeys, nq), lambda h, b: (layer, 0, h, 0, 0)),
        ],
        out_specs=pl.BlockSpec((1, seq, HEAD_DIM), lambda h, b: (h, b, 0)),
        out_shape=jax.ShapeDtypeStruct((NA_HEADS, m, HEAD_DIM), BF16),
        scratch_shapes=[pltpu.VMEM((4, nkeys, nq), F32), pltpu.VMEM((4, 8, nq), F32)],
        compiler_params=_cparams(("parallel", "parallel"), 48),
        name="na",
    )(p, p, p, bias)


def _na_bias(rpb, seq):
    rows = seq // GRID_W
    kr = min(NA_WIN_R, rows)
    ndr = 2 * NA_WIN_R - 1
    ndc = 2 * NA_WIN_C - 1
    c = np.arange(GRID_W)[:, None]
    j = np.arange(GRID_W)[None, :]
    cs = np.clip(j - NA_WIN_C // 2, 0, GRID_W - NA_WIN_C)
    col_ok = (c >= cs) & (c < cs + NA_WIN_C)
    dc = np.clip(c - j + NA_WIN_C - 1, 0, ndc - 1)
    onehot = (dc[:, :, None] == np.arange(ndc)[None, None, :]).astype(np.float32)
    tiles = jnp.einsum('lhrd,cjd->lhrcj', rpb.astype(F32), jnp.asarray(onehot),
                       precision=lax.Precision.HIGHEST)
    tiles = jnp.where(jnp.asarray(col_ok), tiles * LOG2E, NEG_INF)
    masked = jnp.full(tiles.shape[:2] + (1, GRID_W, GRID_W), NEG_INF, F32)
    tiles = jnp.concatenate([tiles, masked], axis=2)
    index = []
    for r0, k0 in ((0, 0), (NA_R, NA_R - kr // 2), (rows - NA_R, rows - NA_KW)):
        r = r0 + np.arange(NA_R)[None, :]
        krow = k0 + np.arange(NA_KW)[:, None]
        start = np.clip(r - kr // 2, 0, rows - kr)
        row_ok = (krow >= start) & (krow < start + kr)
        index.append(np.where(row_ok, krow - r + NA_WIN_R - 1, ndr))
    index = np.stack(index).reshape(-1)
    sel = jnp.take(tiles, jnp.asarray(index), axis=2)
    sel = sel.reshape(sel.shape[:2] + (3, NA_KW, NA_R, GRID_W, GRID_W))
    sel = sel.transpose(0, 2, 1, 3, 5, 4, 6)
    return sel.reshape(sel.shape[:3] + (NA_KW * GRID_W, NA_R * GRID_W))


def _dil_windows(seq):
    out = []
    for win, _ in DIL_GROUPS:
        hw = -(-(win // 2) // DIL_ALIGN) * DIL_ALIGN
        out.append((hw, min(2 * hw + DIL_T, seq)))
    return out


def _dil_kernel(q0_ref, q1_ref, q2_ref, k0_ref, k1_ref, k2_ref, v0_ref, v1_ref, v2_ref,
                e0_ref, e1_ref, e2_ref, o_ref, st_sc, mx_sc, pt_sc):
    q_refs = (q0_ref, q1_ref, q2_ref)
    k_refs = (k0_ref, k1_ref, k2_ref)
    v_refs = (v0_ref, v1_ref, v2_ref)
    e_refs = (e0_ref, e1_ref, e2_ref)
    seq = k0_ref.shape[1]
    nsub = q0_ref.shape[1] // DIL_T
    windows = _dil_windows(seq)
    offs = [sum(w for _, w in windows[:g]) for g in range(len(windows))]

    def key_start(sub, g):
        t0 = pl.program_id(2) * q0_ref.shape[1] + sub * DIL_T
        hw, wlen = windows[g]
        ws = pl.multiple_of(jnp.clip(t0 - hw, 0, seq - wlen), DIL_ALIGN)
        return ws, pl.multiple_of(wlen - DIL_T - (t0 - ws), DIL_ALIGN)

    def scores(sub, dst, mx_dst):
        m8 = None
        for g, (hw, wlen) in enumerate(windows):
            ws, erow = key_start(sub, g)
            qt = (q_refs[g][0, pl.ds(sub * DIL_T, DIL_T), :].astype(F32) * QSCALE).astype(BF16).T
            for r in range(0, wlen, DIL_ROWS):
                n = min(DIL_ROWS, wlen - r)
                k = k_refs[g][0, pl.ds(ws + r, n), :]
                st = (jnp.dot(k, qt, preferred_element_type=F32)
                      + e_refs[g][0, pl.ds(erow + r, n), :])
                dst[pl.ds(offs[g] + r, n), :] = st
                m = jnp.max(st.reshape(n // 8, 8, DIL_T), axis=0)
                m8 = m if m8 is None else jnp.maximum(m8, m)
        mx_dst[...] = m8

    def update(sub, src, mx_src, pt_dst):
        mx = jnp.max(mx_src[...], axis=0, keepdims=True)
        nkeys = src.shape[0]
        l8 = jnp.zeros((8, DIL_T), F32)
        for r in range(0, nkeys, DIL_ROWS):
            n = min(DIL_ROWS, nkeys - r)
            pt = jnp.exp2(src[pl.ds(r, n), :] - mx)
            l8 = l8 + jnp.sum(pt.reshape(n // 8, 8, DIL_T), axis=0)
            pt_dst[pl.ds(r, n), :] = pt.astype(BF16)
        l = jnp.sum(l8, axis=0, keepdims=True)
        acc = jnp.zeros((HEAD_DIM, DIL_T), F32)
        for g, (hw, wlen) in enumerate(windows):
            ws, _ = key_start(sub, g)
            vt = v_refs[g][0, pl.ds(ws, wlen), :].T
            acc = acc + jnp.dot(vt, pt_dst[pl.ds(offs[g], wlen), :], preferred_element_type=F32)
        o_ref[0, pl.ds(sub * DIL_T, DIL_T), :] = (acc / l).T.astype(BF16)

    bufs = [(st_sc.at[i], mx_sc.at[i]) for i in range(4)]
    scores(0, *bufs[0])
    scores(1, *bufs[1])
    for sub in range(0, nsub, 2):
        if sub + 2 < nsub:
            scores(sub + 2, *bufs[(sub + 2) % 4])
            scores(sub + 3, *bufs[(sub + 3) % 4])
        update(sub, *bufs[sub % 4], pt_sc.at[0])
        update(sub + 1, *bufs[(sub + 1) % 4], pt_sc.at[1])


def _dil(p, tables, bsz, seq):
    m = p.shape[1]
    tq = min(DIL_Q, seq)
    nq = seq // tq
    hg = DIL_HEADS_PER_GROUP
    nkeys = sum(w for _, w in _dil_windows(seq))

    def qspec(g):
        return pl.BlockSpec((1, tq, HEAD_DIM), lambda b, h, i: (C_QC + hg * g + h, b * nq + i, 0))

    def kvspec(base, g):
        return _resident((1, seq, HEAD_DIM), lambda b, h, i: (base + hg * g + h, b, 0))

    def tspec(t):
        return _resident((1,) + t.shape[1:], lambda b, h, i: (h, 0, 0))

    return pl.pallas_call(
        _dil_kernel,
        grid=(bsz, hg, nq),
        in_specs=[qspec(0), qspec(1), qspec(2),
                  kvspec(C_KC, 0), kvspec(C_KC, 1), kvspec(C_KC, 2),
                  kvspec(C_VC, 0), kvspec(C_VC, 1), kvspec(C_VC, 2),
                  tspec(tables[0]), tspec(tables[1]), tspec(tables[2])],
        out_specs=pl.BlockSpec((1, tq, HEAD_DIM), lambda b, h, i: (h, b * nq + i, 0)),
        out_shape=jax.ShapeDtypeStruct((hg, m, HEAD_DIM), BF16),
        scratch_shapes=[pltpu.VMEM((4, nkeys, DIL_T), F32), pltpu.VMEM((4, 8, DIL_T), F32),
                        pltpu.VMEM((2, nkeys, DIL_T), BF16)],
        compiler_params=_cparams(("parallel", "parallel", "arbitrary"), 56),
        name="dilated",
    )(p, p, p, p, p, p, p, p, p, *tables)


def _dil_tables(seq):
    slopes = 2.0 ** (-ALIBI_MAX_EXP * jnp.arange(1, DIL_HEADS + 1, dtype=F32) / DIL_HEADS)
    out = []
    for g, (hw, wlen) in enumerate(_dil_windows(seq)):
        win, dil = DIL_GROUPS[g]
        r = np.arange(2 * wlen - DIL_T)[:, None]
        i = np.arange(DIL_T)[None, :]
        rel = i - r + wlen - DIL_T
        ok = jnp.asarray((np.abs(rel) <= win // 2) & (rel % dil == 0))
        dist = jnp.asarray(np.abs(rel), F32)
        sl = slopes[DIL_HEADS_PER_GROUP * g:DIL_HEADS_PER_GROUP * (g + 1)] * LOG2E
        out.append(jnp.where(ok[None], -sl[:, None, None] * dist[None], NEG_INF))
    return out


def _merge_kernel(ya_ref, yb_ref, yc_ref, za0_ref, za1_ref, zb0_ref, zb1_ref, zc_ref, ga_ref, gb_ref, gc_ref,
                  bg_ref, x_ref, wa_ref, wb_ref, wc_ref, wo_ref, pg_ref, o_ref):
    d = x_ref.shape[1]

    def cat(*refs):
        return jnp.concatenate([r[c] for r in refs for c in range(r.shape[0])], axis=-1).astype(F32)

    def branch(y_ref, z_refs, w_ref):
        z = cat(*z_refs)
        u = (cat(y_ref) * (z * jax.nn.sigmoid(z))).astype(BF16)
        return jnp.dot(u, w_ref[...], preferred_element_type=F32)

    def gate(g_ref, k):
        return jax.nn.sigmoid(cat(g_ref) + bg_ref[:, k * d:(k + 1) * d])

    merged = (gate(ga_ref, 0) * branch(ya_ref, (za0_ref, za1_ref), wa_ref)
              + gate(gb_ref, 1) * branch(yb_ref, (zb0_ref, zb1_ref), wb_ref)
              + gate(gc_ref, 2) * branch(yc_ref, (zc_ref,), wc_ref))
    out = jnp.dot(merged.astype(BF16), wo_ref[...], preferred_element_type=F32)
    ms = jnp.mean(out * out, axis=-1, keepdims=True)
    o_ref[...] = x_ref[...] + out * lax.rsqrt(ms + EPS) * pg_ref[...]


def _merge(ya, yb, yc, p, bg, x2d, wa, wb, wc, wo, pg, layer, tm):
    m, d = x2d.shape
    gch = d // V7X_LANES

    zh = NA_HEADS // 2

    def chunks(n, first):
        assert first % n == 0
        return pl.BlockSpec((n, tm, HEAD_DIM), lambda i: (first // n, i, 0))

    def const(shape):
        return _resident(shape, lambda i: (0, 0))

    def weight(w):
        return _resident((None,) + w.shape[1:], lambda i: (layer, 0, 0))

    return pl.pallas_call(
        _merge_kernel,
        grid=(m // tm,),
        in_specs=[
            chunks(NA_HEADS, 0), chunks(GQ_HEADS, 0), chunks(DIL_HEADS_PER_GROUP, 0),
            chunks(zh, C_ZA), chunks(zh, C_ZA + zh), chunks(zh, C_ZB), chunks(zh, C_ZB + zh),
            chunks(DIL_HEADS_PER_GROUP, C_ZC),
            chunks(gch, C_GATE), chunks(gch, C_GATE + gch), chunks(gch, C_GATE + 2 * gch),
            const((1, 3 * d)),
            pl.BlockSpec((tm, d), lambda i: (i, 0)),
            weight(wa), weight(wb), weight(wc), weight(wo),
            const((1, d)),
        ],
        out_specs=pl.BlockSpec((tm, d), lambda i: (i, 0)),
        out_shape=jax.ShapeDtypeStruct((m, d), F32),
        compiler_params=_cparams(("parallel",), 56),
        name="merge",
    )(ya, yb, yc, p, p, p, p, p, p, p, p, bg.reshape(1, 3 * d), x2d, wa, wb, wc, wo, pg.reshape(1, d))


def _layer(x2d, layer, bsz, seq, pre_g, w_in, b_gate, q_g, k_g, na_bias, wa, wb, wc, wo, post_g, cos, sin,
           tables):
    p = _inproj(x2d, pre_g, w_in, layer, tm=min(1024, x2d.shape[0]))
    nprep = (GQ_HEADS + GKV_HEADS) // 2
    gains = jnp.concatenate([jnp.broadcast_to(q_g * QSCALE, (GQ_HEADS // 2, HEAD_DIM)),
                             jnp.broadcast_to(k_g, (GKV_HEADS // 2, HEAD_DIM))], axis=0)
    qk = _qkprep(p, gains.reshape(nprep, 1, HEAD_DIM), cos, sin, seq, tr=min(2048, seq))
    ya = _na(p, na_bias, layer, bsz, seq)
    yb = _gqa(qk, p, bsz, seq, tq=min(256, seq), tk=min(512, seq // GQA_UNROLL),
              ntiles=min(GQA_TILES, seq // min(256, seq)))
    yc = _dil(p, tables, bsz, seq)
    return _merge(ya, yb, yc, p, b_gate, x2d, wa, wb, wc, wo, post_g, layer, tm=256)


def kernel(x, pre_norm_g, w_in, b_gate, q_norm_g, k_norm_g, rpb, w_branch_a, w_branch_b, w_branch_c, w_out,
           post_norm_g):
    bsz, seq, d = x.shape
    depth = w_in.shape[0]
    cos, sin = _rope_tables(seq)
    tables = _dil_tables(seq)
    na_bias = _na_bias(rpb, seq)
    w_in, wa, wb, wc, wo = (w.astype(BF16) for w in (w_in, w_branch_a, w_branch_b, w_branch_c, w_out))
    x2d = x.reshape(bsz * seq, d)
    for l in range(depth):
        x2d = _layer(x2d, l, bsz, seq, pre_norm_g[l], w_in, b_gate[l], q_norm_g[l], k_norm_g[l], na_bias,
                     wa, wb, wc, wo, post_norm_g[l], cos, sin, tables)
    return x2d.reshape(bsz, seq, d)
```

```python
import functools
import math

import numpy as np
import jax
import jax.numpy as jnp
from jax import lax
from jax.experimental import pallas as pl
from jax.experimental.pallas import tpu as pltpu

F32 = jnp.float32
BF16 = jnp.bfloat16

HEAD_DIM = 128
GRID_W = 64
EPS = 1e-6
NEG_INF = -1e30
NA_HEADS = 8
NA_WIN_R = 8
NA_WIN_C = 16
GQ_HEADS = 8
GKV_HEADS = 2
ROPE_BASE = 10000.0
DIL_GROUPS = ((128, 1), (512, 4), (2048, 16))
DIL_HEADS_PER_GROUP = 4
DIL_HEADS = 12
ALIBI_MAX_EXP = 8.0
LOG2E = math.log2(math.e)
QSCALE = HEAD_DIM ** -0.5 * LOG2E

V7X_LANES = 128
MIB = 1024 * 1024

CHUNKS_PER_TILE = 20
SUB_CHUNKS = 4
N_CHUNKS = 140
N_TILES = N_CHUNKS // CHUNKS_PER_TILE
TILE_ROT = 3
C_VC = 0
C_ZA, C_ZB, C_ZC = 12, 20, 28
C_GATE = 32
C_QA, C_KA, C_VA = 80, 88, 96
C_QB, C_KB, C_VB = 104, 112, 114
C_QC, C_KC = 116, 128

NA_R = 4
NA_KW = NA_R + NA_WIN_R - 1
GQA_UNROLL = 4
GQA_TILES = 4
DIL_T = 256
DIL_Q = 8 * DIL_T
DIL_ROWS = 256
DIL_ALIGN = 64


def _cparams(sem, vmem_mib):
    return pltpu.CompilerParams(dimension_semantics=sem, vmem_limit_bytes=int(vmem_mib * MIB))


def _resident(shape, index_map):
    return pl.BlockSpec(shape, index_map, pipeline_mode=pl.Buffered(1))


def _inproj_kernel(x_ref, g_ref, w_ref, p_ref, xn_ref):
    @pl.when(pl.program_id(1) == 0)
    def _():
        x = x_ref[...]
        ms = jnp.mean(x * x, axis=-1, keepdims=True)
        xn_ref[...] = (x * lax.rsqrt(ms + EPS) * g_ref[...]).astype(BF16)

    sub_w = SUB_CHUNKS * V7X_LANES
    for s in range(CHUNKS_PER_TILE // SUB_CHUNKS):
        res = jnp.dot(xn_ref[...], w_ref[:, s * sub_w:(s + 1) * sub_w], preferred_element_type=F32)
        for c in range(SUB_CHUNKS):
            p_ref[s * SUB_CHUNKS + c] = res[:, c * V7X_LANES:(c + 1) * V7X_LANES].astype(BF16)


def _inproj(x2d, g, w, layer, tm):
    m, d = x2d.shape
    tn = CHUNKS_PER_TILE * V7X_LANES
    return pl.pallas_call(
        _inproj_kernel,
        grid=(m // tm, N_TILES),
        in_specs=[
            pl.BlockSpec((tm, d), lambda i, j: (i, 0)),
            pl.BlockSpec((1, d), lambda i, j: (0, 0)),
            pl.BlockSpec((None, d, tn), lambda i, j: (layer, 0, (j + TILE_ROT) % N_TILES)),
        ],
        out_specs=pl.BlockSpec((CHUNKS_PER_TILE, tm, V7X_LANES), lambda i, j: (j, i, 0)),
        out_shape=jax.ShapeDtypeStruct((N_CHUNKS, m, V7X_LANES), BF16),
        scratch_shapes=[pltpu.VMEM((tm, d), BF16)],
        compiler_params=_cparams(("parallel", "arbitrary"), 58),
        name="inproj",
    )(x2d, g.reshape(1, d), w)


def _qkprep_kernel(p_ref, gn_ref, cos_ref, sin_ref, perm_ref, o_ref):
    nc, tr, e = p_ref.shape
    cos = cos_ref[...]
    sin = sin_ref[...]
    gn = gn_ref[0]
    perm = perm_ref[...]
    for c in range(nc):
        x = p_ref[c].astype(F32)
        ms = jnp.mean(x * x, axis=-1, keepdims=True)
        y = x * lax.rsqrt(ms + EPS) * gn
        y_hi = y.astype(BF16)
        y_lo = (y - y_hi.astype(F32)).astype(BF16)
        partner = (jnp.dot(y_hi, perm, preferred_element_type=F32)
                   + jnp.dot(y_lo, perm, preferred_element_type=F32))
        o_ref[c] = (y * cos + partner * sin).astype(BF16)


def _rope_partner_matrix():
    quarter = HEAD_DIM // 4
    j = np.arange(HEAD_DIM)
    src = np.where((j % (2 * quarter)) < quarter, j + quarter, j - quarter)
    perm = np.zeros((HEAD_DIM, HEAD_DIM), np.float32)
    perm[src, j] = 1.0
    return jnp.asarray(perm, BF16)


def _qkprep(p, gains, cos, sin, seq, tr):
    m = p.shape[1]
    nseq = seq // tr
    nblk = (GQ_HEADS + GKV_HEADS) // 2
    return pl.pallas_call(
        _qkprep_kernel,
        grid=(nblk, m // tr),
        in_specs=[
            pl.BlockSpec((2, tr, HEAD_DIM), lambda c, i: (C_QB // 2 + c, i, 0)),
            pl.BlockSpec((1, 1, HEAD_DIM), lambda c, i: (c, 0, 0)),
            pl.BlockSpec((tr, HEAD_DIM), lambda c, i: (i % nseq, 0)),
            pl.BlockSpec((tr, HEAD_DIM), lambda c, i: (i % nseq, 0)),
            pl.BlockSpec((HEAD_DIM, HEAD_DIM), lambda c, i: (0, 0)),
        ],
        out_specs=pl.BlockSpec((2, tr, HEAD_DIM), lambda c, i: (c, i, 0)),
        out_shape=jax.ShapeDtypeStruct((GQ_HEADS + GKV_HEADS, m, HEAD_DIM), BF16),
        compiler_params=_cparams(("parallel", "parallel"), 48),
        name="qkprep",
    )(p, gains, cos, sin, _rope_partner_matrix())


def _rope_tables(seq):
    quarter = HEAD_DIM // 4
    freqs = ROPE_BASE ** (-jnp.arange(quarter, dtype=F32) / quarter)
    t = jnp.arange(seq)
    ang_r = (t // GRID_W).astype(F32)[:, None] * freqs[None, :]
    ang_c = (t % GRID_W).astype(F32)[:, None] * freqs[None, :]
    cos = jnp.concatenate([jnp.cos(ang_r), jnp.cos(ang_r), jnp.cos(ang_c), jnp.cos(ang_c)], axis=-1)
    sin = jnp.concatenate([-jnp.sin(ang_r), jnp.sin(ang_r), -jnp.sin(ang_c), jnp.sin(ang_c)], axis=-1)
    return cos, sin


def _gqa_kernel(q_ref, k_ref, v_ref, o_ref, qt_sc, st0_sc, st1_sc, mx0_sc, mx1_sc, m_sc, l_sc, acc_sc, *, tq, tk):
    grp, tq_step, e = q_ref.shape
    seq = k_ref.shape[1]
    ntiles = tq_step // tq
    nq = grp * tq
    nchunks = seq // tk
    for t in range(ntiles):
        qt_sc[t] = q_ref[:, pl.ds(t * tq, tq), :].reshape(nq, e).T

    def scores(t, i, bufs):
        dst, mx_dst = bufs
        start = pl.multiple_of(i * tk, tk)
        st = jnp.dot(k_ref[0, pl.ds(start, tk), :], qt_sc[t], preferred_element_type=F32)
        dst[...] = st
        mx_dst[...] = jnp.max(st.reshape(tk // 8, 8, nq), axis=0)

    def update(i, bufs):
        src, mx_src = bufs
        start = pl.multiple_of(i * tk, tk)
        vt = v_ref[0, pl.ds(start, tk), :].T
        st = src[...]
        m_old = m_sc[...]
        m_new = jnp.maximum(m_old, jnp.max(mx_src[...], axis=0, keepdims=True))
        alpha = jnp.exp2(m_old - m_new)
        pt = jnp.exp2(st - m_new)
        l_sc[...] = alpha * l_sc[...] + jnp.sum(pt, axis=0, keepdims=True)
        acc_sc[...] = alpha * acc_sc[...] + jnp.dot(vt, pt.astype(BF16), preferred_element_type=F32)
        m_sc[...] = m_new

    bufs = ((st0_sc, mx0_sc), (st1_sc, mx1_sc))
    scores(0, 0, bufs[0])
    for t in range(ntiles):
        m_sc[...] = jnp.full(m_sc.shape, -jnp.inf, F32)
        l_sc[...] = jnp.zeros(l_sc.shape, F32)
        acc_sc[...] = jnp.zeros(acc_sc.shape, F32)

        def quad(j, carry, t=t):
            for u in range(GQA_UNROLL):
                scores(t, GQA_UNROLL * j + u + 1, bufs[(u + 1) % 2])
                update(GQA_UNROLL * j + u, bufs[u % 2])
            return carry

        lax.fori_loop(0, nchunks // GQA_UNROLL - 1, quad, 0)
        base = nchunks - GQA_UNROLL
        for u in range(GQA_UNROLL):
            if u + 1 < GQA_UNROLL:
                scores(t, base + u + 1, bufs[(u + 1) % 2])
            elif t + 1 < ntiles:
                scores(t + 1, 0, bufs[(u + 1) % 2])
            update(base + u, bufs[u % 2])
        o = (acc_sc[...] / l_sc[...]).T
        o_ref[:, pl.ds(t * tq, tq), :] = o.astype(BF16).reshape(grp, tq, e)


def _gqa(qk, p, bsz, seq, tq, tk, ntiles):
    m = qk.shape[1]
    grp = GQ_HEADS // GKV_HEADS
    tq_step = tq * ntiles
    nq = seq // tq_step
    return pl.pallas_call(
        functools.partial(_gqa_kernel, tq=tq, tk=tk),
        grid=(bsz, GKV_HEADS, nq),
        in_specs=[
            pl.BlockSpec((grp, tq_step, HEAD_DIM), lambda b, g, i: (g, b * nq + i, 0)),
            pl.BlockSpec((1, seq, HEAD_DIM), lambda b, g, i: (GQ_HEADS + g, b, 0)),
            pl.BlockSpec((1, seq, HEAD_DIM), lambda b, g, i: (C_VB + g, b, 0)),
        ],
        out_specs=pl.BlockSpec((grp, tq_step, HEAD_DIM), lambda b, g, i: (g, b * nq + i, 0)),
        out_shape=jax.ShapeDtypeStruct((GQ_HEADS, m, HEAD_DIM), BF16),
        scratch_shapes=[pltpu.VMEM((ntiles, HEAD_DIM, grp * tq), BF16),
                        pltpu.VMEM((tk, grp * tq), F32), pltpu.VMEM((tk, grp * tq), F32),
                        pltpu.VMEM((8, grp * tq), F32), pltpu.VMEM((8, grp * tq), F32),
                        pltpu.VMEM((1, grp * tq), F32), pltpu.VMEM((1, grp * tq), F32),
                        pltpu.VMEM((HEAD_DIM, grp * tq), F32)],
        compiler_params=_cparams(("parallel", "parallel", "arbitrary"), 48),
        name="gqa",
    )(qk, qk, p)


def _na_kernel(q_ref, k_ref, v_ref, bias_ref, o_ref, st_sc, mx_sc):
    seq = q_ref.shape[1]
    nq = NA_R * GRID_W
    nkeys = NA_KW * GRID_W
    nblk = seq // nq
    bufs = [(st_sc.at[i], mx_sc.at[i]) for i in range(4)]

    def window(b):
        lead = (NA_WIN_R // 2) * GRID_W
        if isinstance(b, int):
            if b == 0:
                return 0, 0, 0
            if b == nblk - 1:
                return seq - nq, seq - nkeys, 2
            return b * nq, b * nq - lead, 1
        q0 = pl.multiple_of(b * nq, nq)
        return q0, pl.multiple_of(q0 - lead, GRID_W), 1

    def scores(b, bufs):
        dst, mx_dst = bufs
        q0, k0, variant = window(b)
        qt = (q_ref[0, pl.ds(q0, nq), :].astype(F32) * QSCALE).astype(BF16).T
        k = k_ref[0, pl.ds(k0, nkeys), :]
        st = jnp.dot(k, qt, preferred_element_type=F32) + bias_ref[variant, 0]
        dst[...] = st
        mx_dst[...] = jnp.max(st.reshape(nkeys // 8, 8, nq), axis=0)

    def update(b, bufs):
        src, mx_src = bufs
        q0, k0, _ = window(b)
        st = src[...]
        vt = v_ref[0, pl.ds(k0, nkeys), :].T
        pt = jnp.exp2(st - jnp.max(mx_src[...], axis=0, keepdims=True))
        l = jnp.sum(pt, axis=0, keepdims=True)
        ot = jnp.dot(vt, pt.astype(BF16), preferred_element_type=F32)
        o_ref[0, pl.ds(q0, nq), :] = (ot / l).T.astype(BF16)

    scores(0, bufs[0])
    scores(1, bufs[1])

    def quad(j, carry=0):
        for u in (0, 1):
            b = 4 * j + 2 * u
            scores(b + 2, bufs[(2 * u + 2) % 4])
            scores(b + 3, bufs[(2 * u + 3) % 4])
            update(b, bufs[2 * u])
            update(b + 1, bufs[2 * u + 1])
        return carry

    if nblk > 4:
        quad(0)
        lax.fori_loop(1, nblk // 4 - 1, quad, 0)
    base = nblk - 4
    scores(base + 2, bufs[2])
    scores(base + 3, bufs[3])
    update(base, bufs[0])
    update(base + 1, bufs[1])
    update(base + 2, bufs[2])
    update(base + 3, bufs[3])


def _na(p, bias, layer, bsz, seq):
    m = p.shape[1]
    nq = NA_R * GRID_W
    nkeys = NA_KW * GRID_W
    return pl.pallas_call(
        _na_kernel,
        grid=(NA_HEADS, bsz),
        in_specs=[
            pl.BlockSpec((1, seq, HEAD_DIM), lambda h, b: (C_QA + h, b, 0)),
            pl.BlockSpec((1, seq, HEAD_DIM), lambda h, b: (C_KA + h, b, 0)),
            pl.BlockSpec((1, seq, HEAD_DIM), lambda h, b: (C_VA + h, b, 0)),
            pl.BlockSpec((None, 3, 1, nkeys, nq), lambda h, b: (layer, 0, h, 0, 0)),
        ],
        out_specs=pl.BlockSpec((1, seq, HEAD_DIM), lambda h, b: (h, b, 0)),
        out_shape=jax.ShapeDtypeStruct((NA_HEADS, m, HEAD_DIM), BF16),
        scratch_shapes=[pltpu.VMEM((4, nkeys, nq), F32), pltpu.VMEM((4, 8, nq), F32)],
        compiler_params=_cparams(("parallel", "parallel"), 48),
        name="na",
    )(p, p, p, bias)


def _na_bias(rpb, seq):
    rows = seq // GRID_W
    kr = min(NA_WIN_R, rows)
    ndr = 2 * NA_WIN_R - 1
    ndc = 2 * NA_WIN_C - 1
    c = np.arange(GRID_W)[:, None]
    j = np.arange(GRID_W)[None, :]
    cs = np.clip(j - NA_WIN_C // 2, 0, GRID_W - NA_WIN_C)
    col_ok = (c >= cs) & (c < cs + NA_WIN_C)
    dc = np.clip(c - j + NA_WIN_C - 1, 0, ndc - 1)
    onehot = (dc[:, :, None] == np.arange(ndc)[None, None, :]).astype(np.float32)
    tiles = jnp.einsum('lhrd,cjd->lhrcj', rpb.astype(F32), jnp.asarray(onehot),
                       precision=lax.Precision.HIGHEST)
    tiles = jnp.where(jnp.asarray(col_ok), tiles * LOG2E, NEG_INF)
    masked = jnp.full(tiles.shape[:2] + (1, GRID_W, GRID_W), NEG_INF, F32)
    tiles = jnp.concatenate([tiles, masked], axis=2)
    index = []
    for r0, k0 in ((0, 0), (NA_R, NA_R - kr // 2), (rows - NA_R, rows - NA_KW)):
        r = r0 + np.arange(NA_R)[None, :]
        krow = k0 + np.arange(NA_KW)[:, None]
        start = np.clip(r - kr // 2, 0, rows - kr)
        row_ok = (krow >= start) & (krow < start + kr)
        index.append(np.where(row_ok, krow - r + NA_WIN_R - 1, ndr))
    index = np.stack(index).reshape(-1)
    sel = jnp.take(tiles, jnp.asarray(index), axis=2)
    sel = sel.reshape(sel.shape[:2] + (3, NA_KW, NA_R, GRID_W, GRID_W))
    sel = sel.transpose(0, 2, 1, 3, 5, 4, 6)
    return sel.reshape(sel.shape[:3] + (NA_KW * GRID_W, NA_R * GRID_W))


def _dil_windows(seq):
    out = []
    for win, _ in DIL_GROUPS:
        hw = -(-(win // 2) // DIL_ALIGN) * DIL_ALIGN
        out.append((hw, min(2 * hw + DIL_T, seq)))
    return out


def _dil_kernel(q0_ref, q1_ref, q2_ref, k0_ref, k1_ref, k2_ref, v0_ref, v1_ref, v2_ref,
                e0_ref, e1_ref, e2_ref, o_ref, st_sc, mx_sc, pt_sc):
    q_refs = (q0_ref, q1_ref, q2_ref)
    k_refs = (k0_ref, k1_ref, k2_ref)
    v_refs = (v0_ref, v1_ref, v2_ref)
    e_refs = (e0_ref, e1_ref, e2_ref)
    seq = k0_ref.shape[1]
    nsub = q0_ref.shape[1] // DIL_T
    windows = _dil_windows(seq)
    offs = [sum(w for _, w in windows[:g]) for g in range(len(windows))]

    def key_start(sub, g):
        t0 = pl.program_id(2) * q0_ref.shape[1] + sub * DIL_T
        hw, wlen = windows[g]
        ws = pl.multiple_of(jnp.clip(t0 - hw, 0, seq - wlen), DIL_ALIGN)
        return ws, pl.multiple_of(wlen - DIL_T - (t0 - ws), DIL_ALIGN)

    def scores(sub, dst, mx_dst):
        m8 = None
        for g, (hw, wlen) in enumerate(windows):
            ws, erow = key_start(sub, g)
            qt = (q_refs[g][0, pl.ds(sub * DIL_T, DIL_T), :].astype(F32) * QSCALE).astype(BF16).T
            for r in range(0, wlen, DIL_ROWS):
                n = min(DIL_ROWS, wlen - r)
                k = k_refs[g][0, pl.ds(ws + r, n), :]
                st = (jnp.dot(k, qt, preferred_element_type=F32)
                      + e_refs[g][0, pl.ds(erow + r, n), :])
                dst[pl.ds(offs[g] + r, n), :] = st
                m = jnp.max(st.reshape(n // 8, 8, DIL_T), axis=0)
                m8 = m if m8 is None else jnp.maximum(m8, m)
        mx_dst[...] = m8

    def update(sub, src, mx_src, pt_dst):
        mx = jnp.max(mx_src[...], axis=0, keepdims=True)
        nkeys = src.shape[0]
        l8 = jnp.zeros((8, DIL_T), F32)
        for r in range(0, nkeys, DIL_ROWS):
            n = min(DIL_ROWS, nkeys - r)
            pt = jnp.exp2(src[pl.ds(r, n), :] - mx)
            l8 = l8 + jnp.sum(pt.reshape(n // 8, 8, DIL_T), axis=0)
            pt_dst[pl.ds(r, n), :] = pt.astype(BF16)
        l = jnp.sum(l8, axis=0, keepdims=True)
        acc = jnp.zeros((HEAD_DIM, DIL_T), F32)
        for g, (hw, wlen) in enumerate(windows):
            ws, _ = key_start(sub, g)
            vt = v_refs[g][0, pl.ds(ws, wlen), :].T
            acc = acc + jnp.dot(vt, pt_dst[pl.ds(offs[g], wlen), :], preferred_element_type=F32)
        o_ref[0, pl.ds(sub * DIL_T, DIL_T), :] = (acc / l).T.astype(BF16)

    bufs = [(st_sc.at[i], mx_sc.at[i]) for i in range(4)]
    scores(0, *bufs[0])
    scores(1, *bufs[1])
    for sub in range(0, nsub, 2):
        if sub + 2 < nsub:
            scores(sub + 2, *bufs[(sub + 2) % 4])
            scores(sub + 3, *bufs[(sub + 3) % 4])
        update(sub, *bufs[sub % 4], pt_sc.at[0])
        update(sub + 1, *bufs[(sub + 1) % 4], pt_sc.at[1])


def _dil(p, tables, bsz, seq):
    m = p.shape[1]
    tq = min(DIL_Q, seq)
    nq = seq // tq
    hg = DIL_HEADS_PER_GROUP
    nkeys = sum(w for _, w in _dil_windows(seq))

    def qspec(g):
        return pl.BlockSpec((1, tq, HEAD_DIM), lambda b, h, i: (C_QC + hg * g + h, b * nq + i, 0))

    def kvspec(base, g):
        return _resident((1, seq, HEAD_DIM), lambda b, h, i: (base + hg * g + h, b, 0))

    def tspec(t):
        return _resident((1,) + t.shape[1:], lambda b, h, i: (h, 0, 0))

    return pl.pallas_call(
        _dil_kernel,
        grid=(bsz, hg, nq),
        in_specs=[qspec(0), qspec(1), qspec(2),
                  kvspec(C_KC, 0), kvspec(C_KC, 1), kvspec(C_KC, 2),
                  kvspec(C_VC, 0), kvspec(C_VC, 1), kvspec(C_VC, 2),
                  tspec(tables[0]), tspec(tables[1]), tspec(tables[2])],
        out_specs=pl.BlockSpec((1, tq, HEAD_DIM), lambda b, h, i: (h, b * nq + i, 0)),
        out_shape=jax.ShapeDtypeStruct((hg, m, HEAD_DIM), BF16),
        scratch_shapes=[pltpu.VMEM((4, nkeys, DIL_T), F32), pltpu.VMEM((4, 8, DIL_T), F32),
                        pltpu.VMEM((2, nkeys, DIL_T), BF16)],
        compiler_params=_cparams(("parallel", "parallel", "arbitrary"), 56),
        name="dilated",
    )(p, p, p, p, p, p, p, p, p, *tables)


def _dil_tables(seq):
    slopes = 2.0 ** (-ALIBI_MAX_EXP * jnp.arange(1, DIL_HEADS + 1, dtype=F32) / DIL_HEADS)
    out = []
    for g, (hw, wlen) in enumerate(_dil_windows(seq)):
        win, dil = DIL_GROUPS[g]
        r = np.arange(2 * wlen - DIL_T)[:, None]
        i = np.arange(DIL_T)[None, :]
        rel = i - r + wlen - DIL_T
        ok = jnp.asarray((np.abs(rel) <= win // 2) & (rel % dil == 0))
        dist = jnp.asarray(np.abs(rel), F32)
        sl = slopes[DIL_HEADS_PER_GROUP * g:DIL_HEADS_PER_GROUP * (g + 1)] * LOG2E
        out.append(jnp.where(ok[None], -sl[:, None, None] * dist[None], NEG_INF))
    return out


def _merge_kernel(ya_ref, yb_ref, yc_ref, za0_ref, za1_ref, zb0_ref, zb1_ref, zc_ref, ga_ref, gb_ref, gc_ref,
                  bg_ref, x_ref, wa_ref, wb_ref, wc_ref, wo_ref, pg_ref, o_ref):
    d = x_ref.shape[1]

    def cat(*refs):
        return jnp.concatenate([r[c] for r in refs for c in range(r.shape[0])], axis=-1).astype(F32)

    def branch(y_ref, z_refs, w_ref):
        z = cat(*z_refs)
        u = (cat(y_ref) * (z * jax.nn.sigmoid(z))).astype(BF16)
        return jnp.dot(u, w_ref[...], preferred_element_type=F32)

    def gate(g_ref, k):
        return jax.nn.sigmoid(cat(g_ref) + bg_ref[:, k * d:(k + 1) * d])

    merged = (gate(ga_ref, 0) * branch(ya_ref, (za0_ref, za1_ref), wa_ref)
              + gate(gb_ref, 1) * branch(yb_ref, (zb0_ref, zb1_ref), wb_ref)
              + gate(gc_ref, 2) * branch(yc_ref, (zc_ref,), wc_ref))
    out = jnp.dot(merged.astype(BF16), wo_ref[...], preferred_element_type=F32)
    ms = jnp.mean(out * out, axis=-1, keepdims=True)
    o_ref[...] = x_ref[...] + out * lax.rsqrt(ms + EPS) * pg_ref[...]


def _merge(ya, yb, yc, p, bg, x2d, wa, wb, wc, wo, pg, layer, tm):
    m, d = x2d.shape
    gch = d // V7X_LANES

    zh = NA_HEADS // 2

    def chunks(n, first):
        assert first % n == 0
        return pl.BlockSpec((n, tm, HEAD_DIM), lambda i: (first // n, i, 0))

    def const(shape):
        return _resident(shape, lambda i: (0, 0))

    def weight(w):
        return _resident((None,) + w.shape[1:], lambda i: (layer, 0, 0))

    return pl.pallas_call(
        _merge_kernel,
        grid=(m // tm,),
        in_specs=[
            chunks(NA_HEADS, 0), chunks(GQ_HEADS, 0), chunks(DIL_HEADS_PER_GROUP, 0),
            chunks(zh, C_ZA), chunks(zh, C_ZA + zh), chunks(zh, C_ZB), chunks(zh, C_ZB + zh),
            chunks(DIL_HEADS_PER_GROUP, C_ZC),
            chunks(gch, C_GATE), chunks(gch, C_GATE + gch), chunks(gch, C_GATE + 2 * gch),
            const((1, 3 * d)),
            pl.BlockSpec((tm, d), lambda i: (i, 0)),
            weight(wa), weight(wb), weight(wc), weight(wo),
            const((1, d)),
        ],
        out_specs=pl.BlockSpec((tm, d), lambda i: (i, 0)),
        out_shape=jax.ShapeDtypeStruct((m, d), F32),
        compiler_params=_cparams(("parallel",), 56),
        name="merge",
    )(ya, yb, yc, p, p, p, p, p, p, p, p, bg.reshape(1, 3 * d), x2d, wa, wb, wc, wo, pg.reshape(1, d))


def _layer(x2d, layer, bsz, seq, pre_g, w_in, b_gate, q_g, k_g, na_bias, wa, wb, wc, wo, post_g, cos, sin,
           tables):
    p = _inproj(x2d, pre_g, w_in, layer, tm=min(1024, x2d.shape[0]))
    nprep = (GQ_HEADS + GKV_HEADS) // 2
    gains = jnp.concatenate([jnp.broadcast_to(q_g * QSCALE, (GQ_HEADS // 2, HEAD_DIM)),
                             jnp.broadcast_to(k_g, (GKV_HEADS // 2, HEAD_DIM))], axis=0)
    qk = _qkprep(p, gains.reshape(nprep, 1, HEAD_DIM), cos, sin, seq, tr=min(4096, seq))
    ya = _na(p, na_bias, layer, bsz, seq)
    yb = _gqa(qk, p, bsz, seq, tq=min(256, seq), tk=min(512, seq // GQA_UNROLL),
              ntiles=min(GQA_TILES, seq // min(256, seq)))
    yc = _dil(p, tables, bsz, seq)
    return _merge(ya, yb, yc, p, b_gate, x2d, wa, wb, wc, wo, post_g, layer, tm=256)


def kernel(x, pre_norm_g, w_in, b_gate, q_norm_g, k_norm_g, rpb, w_branch_a, w_branch_b, w_branch_c, w_out,
           post_norm_g):
    bsz, seq, d = x.shape
    depth = w_in.shape[0]
    cos, sin = _rope_tables(seq)
    tables = _dil_tables(seq)
    na_bias = _na_bias(rpb, seq)
    w_in, wa, wb, wc, wo = (w.astype(BF16) for w in (w_in, w_branch_a, w_branch_b, w_branch_c, w_out))
    x2d = x.reshape(bsz * seq, d)
    for l in range(depth):
        x2d = _layer(x2d, l, bsz, seq, pre_norm_g[l], w_in, b_gate[l], q_norm_g[l], k_norm_g[l], na_bias,
                     wa, wb, wc, wo, post_norm_g[l], cos, sin, tables)
    return x2d.reshape(bsz, seq, d)
```

```python
import functools
import math

import numpy as np
import jax
import jax.numpy as jnp
from jax import lax
from jax.experimental import pallas as pl
from jax.experimental.pallas import tpu as pltpu

F32 = jnp.float32
BF16 = jnp.bfloat16

HEAD_DIM = 128
GRID_W = 64
EPS = 1e-6
NEG_INF = -1e30
NA_HEADS = 8
NA_WIN_R = 8
NA_WIN_C = 16
GQ_HEADS = 8
GKV_HEADS = 2
ROPE_BASE = 10000.0
DIL_GROUPS = ((128, 1), (512, 4), (2048, 16))
DIL_HEADS_PER_GROUP = 4
DIL_HEADS = 12
ALIBI_MAX_EXP = 8.0
LOG2E = math.log2(math.e)
QSCALE = HEAD_DIM ** -0.5 * LOG2E

V7X_LANES = 128
MIB = 1024 * 1024

CHUNKS_PER_TILE = 20
SUB_CHUNKS = 4
N_CHUNKS = 140
N_TILES = N_CHUNKS // CHUNKS_PER_TILE
TILE_ROT = 3
C_VC = 0
C_ZA, C_ZB, C_ZC = 12, 20, 28
C_GATE = 32
C_QA, C_KA, C_VA = 80, 88, 96
C_QB, C_KB, C_VB = 104, 112, 114
C_QC, C_KC = 116, 128

NA_R = 4
NA_KW = NA_R + NA_WIN_R - 1
GQA_UNROLL = 4
GQA_TILES = 4
DIL_T = 256
DIL_Q = 8 * DIL_T
DIL_ROWS = 256
MERGE_COL_BLOCKS = 2
DIL_ALIGN = 64


def _cparams(sem, vmem_mib):
    return pltpu.CompilerParams(dimension_semantics=sem, vmem_limit_bytes=int(vmem_mib * MIB))


def _resident(shape, index_map):
    return pl.BlockSpec(shape, index_map, pipeline_mode=pl.Buffered(1))


def _inproj_kernel(x_ref, g_ref, w_ref, p_ref, xn_ref):
    @pl.when(pl.program_id(1) == 0)
    def _():
        x = x_ref[...]
        ms = jnp.mean(x * x, axis=-1, keepdims=True)
        xn_ref[...] = (x * lax.rsqrt(ms + EPS) * g_ref[...]).astype(BF16)

    sub_w = SUB_CHUNKS * V7X_LANES
    for s in range(CHUNKS_PER_TILE // SUB_CHUNKS):
        res = jnp.dot(xn_ref[...], w_ref[:, s * sub_w:(s + 1) * sub_w], preferred_element_type=F32)
        for c in range(SUB_CHUNKS):
            p_ref[s * SUB_CHUNKS + c] = res[:, c * V7X_LANES:(c + 1) * V7X_LANES].astype(BF16)


def _inproj(x2d, g, w, layer, tm):
    m, d = x2d.shape
    tn = CHUNKS_PER_TILE * V7X_LANES
    return pl.pallas_call(
        _inproj_kernel,
        grid=(m // tm, N_TILES),
        in_specs=[
            pl.BlockSpec((tm, d), lambda i, j: (i, 0)),
            pl.BlockSpec((1, d), lambda i, j: (0, 0)),
            pl.BlockSpec((None, d, tn), lambda i, j: (layer, 0, (j + TILE_ROT) % N_TILES)),
        ],
        out_specs=pl.BlockSpec((CHUNKS_PER_TILE, tm, V7X_LANES), lambda i, j: (j, i, 0)),
        out_shape=jax.ShapeDtypeStruct((N_CHUNKS, m, V7X_LANES), BF16),
        scratch_shapes=[pltpu.VMEM((tm, d), BF16)],
        compiler_params=_cparams(("parallel", "arbitrary"), 58),
        name="inproj",
    )(x2d, g.reshape(1, d), w)


def _qkprep_kernel(p_ref, gn_ref, cos_ref, sin_ref, perm_ref, o_ref):
    nc, tr, e = p_ref.shape
    cos = cos_ref[...]
    sin = sin_ref[...]
    gn = gn_ref[0]
    perm = perm_ref[...]
    for c in range(nc):
        x = p_ref[c].astype(F32)
        ms = jnp.mean(x * x, axis=-1, keepdims=True)
        y = x * lax.rsqrt(ms + EPS) * gn
        y_hi = y.astype(BF16)
        y_lo = (y - y_hi.astype(F32)).astype(BF16)
        partner = (jnp.dot(y_hi, perm, preferred_element_type=F32)
                   + jnp.dot(y_lo, perm, preferred_element_type=F32))
        o_ref[c] = (y * cos + partner * sin).astype(BF16)


def _rope_partner_matrix():
    quarter = HEAD_DIM // 4
    j = np.arange(HEAD_DIM)
    src = np.where((j % (2 * quarter)) < quarter, j + quarter, j - quarter)
    perm = np.zeros((HEAD_DIM, HEAD_DIM), np.float32)
    perm[src, j] = 1.0
    return jnp.asarray(perm, BF16)


def _qkprep(p, gains, cos, sin, seq, tr):
    m = p.shape[1]
    nseq = seq // tr
    nblk = (GQ_HEADS + GKV_HEADS) // 2
    return pl.pallas_call(
        _qkprep_kernel,
        grid=(nblk, m // tr),
        in_specs=[
            pl.BlockSpec((2, tr, HEAD_DIM), lambda c, i: (C_QB // 2 + c, i, 0)),
            pl.BlockSpec((1, 1, HEAD_DIM), lambda c, i: (c, 0, 0)),
            pl.BlockSpec((tr, HEAD_DIM), lambda c, i: (i % nseq, 0)),
            pl.BlockSpec((tr, HEAD_DIM), lambda c, i: (i % nseq, 0)),
            pl.BlockSpec((HEAD_DIM, HEAD_DIM), lambda c, i: (0, 0)),
        ],
        out_specs=pl.BlockSpec((2, tr, HEAD_DIM), lambda c, i: (c, i, 0)),
        out_shape=jax.ShapeDtypeStruct((GQ_HEADS + GKV_HEADS, m, HEAD_DIM), BF16),
        compiler_params=_cparams(("parallel", "parallel"), 48),
        name="qkprep",
    )(p, gains, cos, sin, _rope_partner_matrix())


def _rope_tables(seq):
    quarter = HEAD_DIM // 4
    freqs = ROPE_BASE ** (-jnp.arange(quarter, dtype=F32) / quarter)
    t = jnp.arange(seq)
    ang_r = (t // GRID_W).astype(F32)[:, None] * freqs[None, :]
    ang_c = (t % GRID_W).astype(F32)[:, None] * freqs[None, :]
    cos = jnp.concatenate([jnp.cos(ang_r), jnp.cos(ang_r), jnp.cos(ang_c), jnp.cos(ang_c)], axis=-1)
    sin = jnp.concatenate([-jnp.sin(ang_r), jnp.sin(ang_r), -jnp.sin(ang_c), jnp.sin(ang_c)], axis=-1)
    return cos, sin


def _gqa_kernel(q_ref, k_ref, v_ref, o_ref, qt_sc, st0_sc, st1_sc, mx0_sc, mx1_sc, m_sc, l_sc, acc_sc, *, tq, tk):
    grp, tq_step, e = q_ref.shape
    seq = k_ref.shape[1]
    ntiles = tq_step // tq
    nq = grp * tq
    nchunks = seq // tk
    for t in range(ntiles):
        qt_sc[t] = q_ref[:, pl.ds(t * tq, tq), :].reshape(nq, e).T

    def scores(t, i, bufs):
        dst, mx_dst = bufs
        start = pl.multiple_of(i * tk, tk)
        st = jnp.dot(k_ref[0, pl.ds(start, tk), :], qt_sc[t], preferred_element_type=F32)
        dst[...] = st
        mx_dst[...] = jnp.max(st.reshape(tk // 8, 8, nq), axis=0)

    def update(i, bufs):
        src, mx_src = bufs
        start = pl.multiple_of(i * tk, tk)
        vt = v_ref[0, pl.ds(start, tk), :].T
        st = src[...]
        m_old = m_sc[...]
        m_new = jnp.maximum(m_old, jnp.max(mx_src[...], axis=0, keepdims=True))
        alpha = jnp.exp2(m_old - m_new)
        pt = jnp.exp2(st - m_new)
        l_sc[...] = alpha * l_sc[...] + jnp.sum(pt, axis=0, keepdims=True)
        acc_sc[...] = alpha * acc_sc[...] + jnp.dot(vt, pt.astype(BF16), preferred_element_type=F32)
        m_sc[...] = m_new

    bufs = ((st0_sc, mx0_sc), (st1_sc, mx1_sc))
    scores(0, 0, bufs[0])
    for t in range(ntiles):
        m_sc[...] = jnp.full(m_sc.shape, -jnp.inf, F32)
        l_sc[...] = jnp.zeros(l_sc.shape, F32)
        acc_sc[...] = jnp.zeros(acc_sc.shape, F32)

        def quad(j, carry, t=t):
            for u in range(GQA_UNROLL):
                scores(t, GQA_UNROLL * j + u + 1, bufs[(u + 1) % 2])
                update(GQA_UNROLL * j + u, bufs[u % 2])
            return carry

        lax.fori_loop(0, nchunks // GQA_UNROLL - 1, quad, 0)
        base = nchunks - GQA_UNROLL
        for u in range(GQA_UNROLL):
            if u + 1 < GQA_UNROLL:
                scores(t, base + u + 1, bufs[(u + 1) % 2])
            elif t + 1 < ntiles:
                scores(t + 1, 0, bufs[(u + 1) % 2])
            update(base + u, bufs[u % 2])
        o = (acc_sc[...] / l_sc[...]).T
        o_ref[:, pl.ds(t * tq, tq), :] = o.astype(BF16).reshape(grp, tq, e)


def _gqa(qk, p, bsz, seq, tq, tk, ntiles):
    m = qk.shape[1]
    grp = GQ_HEADS // GKV_HEADS
    tq_step = tq * ntiles
    nq = seq // tq_step
    return pl.pallas_call(
        functools.partial(_gqa_kernel, tq=tq, tk=tk),
        grid=(bsz, GKV_HEADS, nq),
        in_specs=[
            pl.BlockSpec((grp, tq_step, HEAD_DIM), lambda b, g, i: (g, b * nq + i, 0)),
            pl.BlockSpec((1, seq, HEAD_DIM), lambda b, g, i: (GQ_HEADS + g, b, 0)),
            pl.BlockSpec((1, seq, HEAD_DIM), lambda b, g, i: (C_VB + g, b, 0)),
        ],
        out_specs=pl.BlockSpec((grp, tq_step, HEAD_DIM), lambda b, g, i: (g, b * nq + i, 0)),
        out_shape=jax.ShapeDtypeStruct((GQ_HEADS, m, HEAD_DIM), BF16),
        scratch_shapes=[pltpu.VMEM((ntiles, HEAD_DIM, grp * tq), BF16),
                        pltpu.VMEM((tk, grp * tq), F32), pltpu.VMEM((tk, grp * tq), F32),
                        pltpu.VMEM((8, grp * tq), F32), pltpu.VMEM((8, grp * tq), F32),
                        pltpu.VMEM((1, grp * tq), F32), pltpu.VMEM((1, grp * tq), F32),
                        pltpu.VMEM((HEAD_DIM, grp * tq), F32)],
        compiler_params=_cparams(("parallel", "parallel", "arbitrary"), 48),
        name="gqa",
    )(qk, qk, p)


def _na_kernel(q_ref, k_ref, v_ref, bias_ref, o_ref, st_sc, mx_sc):
    seq = q_ref.shape[1]
    nq = NA_R * GRID_W
    nkeys = NA_KW * GRID_W
    nblk = seq // nq
    bufs = [(st_sc.at[i], mx_sc.at[i]) for i in range(4)]

    def window(b):
        lead = (NA_WIN_R // 2) * GRID_W
        if isinstance(b, int):
            if b == 0:
                return 0, 0, 0
            if b == nblk - 1:
                return seq - nq, seq - nkeys, 2
            return b * nq, b * nq - lead, 1
        q0 = pl.multiple_of(b * nq, nq)
        return q0, pl.multiple_of(q0 - lead, GRID_W), 1

    def scores(b, bufs):
        dst, mx_dst = bufs
        q0, k0, variant = window(b)
        qt = (q_ref[0, pl.ds(q0, nq), :].astype(F32) * QSCALE).astype(BF16).T
        k = k_ref[0, pl.ds(k0, nkeys), :]
        st = jnp.dot(k, qt, preferred_element_type=F32) + bias_ref[variant, 0]
        dst[...] = st
        mx_dst[...] = jnp.max(st.reshape(nkeys // 8, 8, nq), axis=0)

    def update(b, bufs):
        src, mx_src = bufs
        q0, k0, _ = window(b)
        st = src[...]
        vt = v_ref[0, pl.ds(k0, nkeys), :].T
        pt = jnp.exp2(st - jnp.max(mx_src[...], axis=0, keepdims=True))
        l = jnp.sum(pt, axis=0, keepdims=True)
        ot = jnp.dot(vt, pt.astype(BF16), preferred_element_type=F32)
        o_ref[0, pl.ds(q0, nq), :] = (ot / l).T.astype(BF16)

    scores(0, bufs[0])
    scores(1, bufs[1])

    def quad(j, carry=0):
        for u in (0, 1):
            b = 4 * j + 2 * u
            scores(b + 2, bufs[(2 * u + 2) % 4])
            scores(b + 3, bufs[(2 * u + 3) % 4])
            update(b, bufs[2 * u])
            update(b + 1, bufs[2 * u + 1])
        return carry

    if nblk > 4:
        quad(0)
        lax.fori_loop(1, nblk // 4 - 1, quad, 0)
    base = nblk - 4
    scores(base + 2, bufs[2])
    scores(base + 3, bufs[3])
    update(base, bufs[0])
    update(base + 1, bufs[1])
    update(base + 2, bufs[2])
    update(base + 3, bufs[3])


def _na(p, bias, layer, bsz, seq):
    m = p.shape[1]
    nq = NA_R * GRID_W
    nkeys = NA_KW * GRID_W
    return pl.pallas_call(
        _na_kernel,
        grid=(NA_HEADS, bsz),
        in_specs=[
            pl.BlockSpec((1, seq, HEAD_DIM), lambda h, b: (C_QA + h, b, 0)),
            pl.BlockSpec((1, seq, HEAD_DIM), lambda h, b: (C_KA + h, b, 0)),
            pl.BlockSpec((1, seq, HEAD_DIM), lambda h, b: (C_VA + h, b, 0)),
            pl.BlockSpec((None, 3, 1, nkeys, nq), lambda h, b: (layer, 0, h, 0, 0)),
        ],
        out_specs=pl.BlockSpec((1, seq, HEAD_DIM), lambda h, b: (h, b, 0)),
        out_shape=jax.ShapeDtypeStruct((NA_HEADS, m, HEAD_DIM), BF16),
        scratch_shapes=[pltpu.VMEM((4, nkeys, nq), F32), pltpu.VMEM((4, 8, nq), F32)],
        compiler_params=_cparams(("parallel", "parallel"), 48),
        name="na",
    )(p, p, p, bias)


def _na_bias(rpb, seq):
    rows = seq // GRID_W
    kr = min(NA_WIN_R, rows)
    ndr = 2 * NA_WIN_R - 1
    ndc = 2 * NA_WIN_C - 1
    c = np.arange(GRID_W)[:, None]
    j = np.arange(GRID_W)[None, :]
    cs = np.clip(j - NA_WIN_C // 2, 0, GRID_W - NA_WIN_C)
    col_ok = (c >= cs) & (c < cs + NA_WIN_C)
    dc = np.clip(c - j + NA_WIN_C - 1, 0, ndc - 1)
    onehot = (dc[:, :, None] == np.arange(ndc)[None, None, :]).astype(np.float32)
    tiles = jnp.einsum('lhrd,cjd->lhrcj', rpb.astype(F32), jnp.asarray(onehot),
                       precision=lax.Precision.HIGHEST)
    tiles = jnp.where(jnp.asarray(col_ok), tiles * LOG2E, NEG_INF)
    masked = jnp.full(tiles.shape[:2] + (1, GRID_W, GRID_W), NEG_INF, F32)
    tiles = jnp.concatenate([tiles, masked], axis=2)
    index = []
    for r0, k0 in ((0, 0), (NA_R, NA_R - kr // 2), (rows - NA_R, rows - NA_KW)):
        r = r0 + np.arange(NA_R)[None, :]
        krow = k0 + np.arange(NA_KW)[:, None]
        start = np.clip(r - kr // 2, 0, rows - kr)
        row_ok = (krow >= start) & (krow < start + kr)
        index.append(np.where(row_ok, krow - r + NA_WIN_R - 1, ndr))
    index = np.stack(index).reshape(-1)
    sel = jnp.take(tiles, jnp.asarray(index), axis=2)
    sel = sel.reshape(sel.shape[:2] + (3, NA_KW, NA_R, GRID_W, GRID_W))
    sel = sel.transpose(0, 2, 1, 3, 5, 4, 6)
    return sel.reshape(sel.shape[:3] + (NA_KW * GRID_W, NA_R * GRID_W))


def _dil_windows(seq):
    out = []
    for win, _ in DIL_GROUPS:
        hw = -(-(win // 2) // DIL_ALIGN) * DIL_ALIGN
        out.append((hw, min(2 * hw + DIL_T, seq)))
    return out


def _dil_kernel(q0_ref, q1_ref, q2_ref, k0_ref, k1_ref, k2_ref, v0_ref, v1_ref, v2_ref,
                e0_ref, e1_ref, e2_ref, o_ref, st_sc, mx_sc, pt_sc):
    q_refs = (q0_ref, q1_ref, q2_ref)
    k_refs = (k0_ref, k1_ref, k2_ref)
    v_refs = (v0_ref, v1_ref, v2_ref)
    e_refs = (e0_ref, e1_ref, e2_ref)
    seq = k0_ref.shape[1]
    nsub = q0_ref.shape[1] // DIL_T
    windows = _dil_windows(seq)
    offs = [sum(w for _, w in windows[:g]) for g in range(len(windows))]

    def key_start(sub, g):
        t0 = pl.program_id(2) * q0_ref.shape[1] + sub * DIL_T
        hw, wlen = windows[g]
        ws = pl.multiple_of(jnp.clip(t0 - hw, 0, seq - wlen), DIL_ALIGN)
        return ws, pl.multiple_of(wlen - DIL_T - (t0 - ws), DIL_ALIGN)

    def scores(sub, dst, mx_dst):
        m8 = None
        for g, (hw, wlen) in enumerate(windows):
            ws, erow = key_start(sub, g)
            qt = (q_refs[g][0, pl.ds(sub * DIL_T, DIL_T), :].astype(F32) * QSCALE).astype(BF16).T
            for r in range(0, wlen, DIL_ROWS):
                n = min(DIL_ROWS, wlen - r)
                k = k_refs[g][0, pl.ds(ws + r, n), :]
                st = (jnp.dot(k, qt, preferred_element_type=F32)
                      + e_refs[g][0, pl.ds(erow + r, n), :])
                dst[pl.ds(offs[g] + r, n), :] = st
                m = jnp.max(st.reshape(n // 8, 8, DIL_T), axis=0)
                m8 = m if m8 is None else jnp.maximum(m8, m)
        mx_dst[...] = m8

    def update(sub, src, mx_src, pt_dst):
        mx = jnp.max(mx_src[...], axis=0, keepdims=True)
        nkeys = src.shape[0]
        l8 = jnp.zeros((8, DIL_T), F32)
        for r in range(0, nkeys, DIL_ROWS):
            n = min(DIL_ROWS, nkeys - r)
            pt = jnp.exp2(src[pl.ds(r, n), :] - mx)
            l8 = l8 + jnp.sum(pt.reshape(n // 8, 8, DIL_T), axis=0)
            pt_dst[pl.ds(r, n), :] = pt.astype(BF16)
        l = jnp.sum(l8, axis=0, keepdims=True)
        acc = jnp.zeros((HEAD_DIM, DIL_T), F32)
        for g, (hw, wlen) in enumerate(windows):
            ws, _ = key_start(sub, g)
            vt = v_refs[g][0, pl.ds(ws, wlen), :].T
            acc = acc + jnp.dot(vt, pt_dst[pl.ds(offs[g], wlen), :], preferred_element_type=F32)
        o_ref[0, pl.ds(sub * DIL_T, DIL_T), :] = (acc / l).T.astype(BF16)

    bufs = [(st_sc.at[i], mx_sc.at[i]) for i in range(4)]
    scores(0, *bufs[0])
    scores(1, *bufs[1])
    for sub in range(0, nsub, 2):
        if sub + 2 < nsub:
            scores(sub + 2, *bufs[(sub + 2) % 4])
            scores(sub + 3, *bufs[(sub + 3) % 4])
        update(sub, *bufs[sub % 4], pt_sc.at[0])
        update(sub + 1, *bufs[(sub + 1) % 4], pt_sc.at[1])


def _dil(p, tables, bsz, seq):
    m = p.shape[1]
    tq = min(DIL_Q, seq)
    nq = seq // tq
    hg = DIL_HEADS_PER_GROUP
    nkeys = sum(w for _, w in _dil_windows(seq))

    def qspec(g):
        return pl.BlockSpec((1, tq, HEAD_DIM), lambda b, h, i: (C_QC + hg * g + h, b * nq + i, 0))

    def kvspec(base, g):
        return _resident((1, seq, HEAD_DIM), lambda b, h, i: (base + hg * g + h, b, 0))

    def tspec(t):
        return _resident((1,) + t.shape[1:], lambda b, h, i: (h, 0, 0))

    return pl.pallas_call(
        _dil_kernel,
        grid=(bsz, hg, nq),
        in_specs=[qspec(0), qspec(1), qspec(2),
                  kvspec(C_KC, 0), kvspec(C_KC, 1), kvspec(C_KC, 2),
                  kvspec(C_VC, 0), kvspec(C_VC, 1), kvspec(C_VC, 2),
                  tspec(tables[0]), tspec(tables[1]), tspec(tables[2])],
        out_specs=pl.BlockSpec((1, tq, HEAD_DIM), lambda b, h, i: (h, b * nq + i, 0)),
        out_shape=jax.ShapeDtypeStruct((hg, m, HEAD_DIM), BF16),
        scratch_shapes=[pltpu.VMEM((4, nkeys, DIL_T), F32), pltpu.VMEM((4, 8, DIL_T), F32),
                        pltpu.VMEM((2, nkeys, DIL_T), BF16)],
        compiler_params=_cparams(("parallel", "parallel", "arbitrary"), 56),
        name="dilated",
    )(p, p, p, p, p, p, p, p, p, *tables)


def _dil_tables(seq):
    slopes = 2.0 ** (-ALIBI_MAX_EXP * jnp.arange(1, DIL_HEADS + 1, dtype=F32) / DIL_HEADS)
    out = []
    for g, (hw, wlen) in enumerate(_dil_windows(seq)):
        win, dil = DIL_GROUPS[g]
        r = np.arange(2 * wlen - DIL_T)[:, None]
        i = np.arange(DIL_T)[None, :]
        rel = i - r + wlen - DIL_T
        ok = jnp.asarray((np.abs(rel) <= win // 2) & (rel % dil == 0))
        dist = jnp.asarray(np.abs(rel), F32)
        sl = slopes[DIL_HEADS_PER_GROUP * g:DIL_HEADS_PER_GROUP * (g + 1)] * LOG2E
        out.append(jnp.where(ok[None], -sl[:, None, None] * dist[None], NEG_INF))
    return out


def _merge_kernel(ya_ref, yb_ref, yc_ref, za0_ref, za1_ref, zb0_ref, zb1_ref, zc_ref, ga_ref, gb_ref, gc_ref,
                  bg_ref, x_ref, wa_ref, wb_ref, wc_ref, wo_ref, pg_ref, o_ref):
    d = x_ref.shape[1]

    def cat(*refs):
        return jnp.concatenate([r[c] for r in refs for c in range(r.shape[0])], axis=-1).astype(F32)

    def silu_path(y_ref, z_refs):
        z = cat(*z_refs)
        return (cat(y_ref) * (z * jax.nn.sigmoid(z))).astype(BF16)

    us = (silu_path(ya_ref, (za0_ref, za1_ref)), silu_path(yb_ref, (zb0_ref, zb1_ref)),
          silu_path(yc_ref, (zc_ref,)))
    width = d // MERGE_COL_BLOCKS
    gch = width // V7X_LANES
    parts = []
    for h in range(MERGE_COL_BLOCKS):
        acc = None
        for k, (u, g_ref, w_ref) in enumerate(zip(us, (ga_ref, gb_ref, gc_ref), (wa_ref, wb_ref, wc_ref))):
            g = jnp.concatenate([g_ref[h * gch + c] for c in range(gch)], axis=-1).astype(F32)
            g = jax.nn.sigmoid(g + bg_ref[:, k * d + h * width:k * d + (h + 1) * width])
            term = g * jnp.dot(u, w_ref[:, h * width:(h + 1) * width], preferred_element_type=F32)
            acc = term if acc is None else acc + term
        parts.append(acc.astype(BF16))
    merged = jnp.concatenate(parts, axis=-1)
    out = jnp.dot(merged, wo_ref[...], preferred_element_type=F32)
    ms = jnp.mean(out * out, axis=-1, keepdims=True)
    o_ref[...] = x_ref[...] + out * lax.rsqrt(ms + EPS) * pg_ref[...]


def _merge(ya, yb, yc, p, bg, x2d, wa, wb, wc, wo, pg, layer, tm):
    m, d = x2d.shape
    gch = d // V7X_LANES

    zh = NA_HEADS // 2

    def chunks(n, first):
        assert first % n == 0
        return pl.BlockSpec((n, tm, HEAD_DIM), lambda i: (first // n, i, 0))

    def const(shape):
        return _resident(shape, lambda i: (0, 0))

    def weight(w):
        return _resident((None,) + w.shape[1:], lambda i: (layer, 0, 0))

    return pl.pallas_call(
        _merge_kernel,
        grid=(m // tm,),
        in_specs=[
            chunks(NA_HEADS, 0), chunks(GQ_HEADS, 0), chunks(DIL_HEADS_PER_GROUP, 0),
            chunks(zh, C_ZA), chunks(zh, C_ZA + zh), chunks(zh, C_ZB), chunks(zh, C_ZB + zh),
            chunks(DIL_HEADS_PER_GROUP, C_ZC),
            chunks(gch, C_GATE), chunks(gch, C_GATE + gch), chunks(gch, C_GATE + 2 * gch),
            const((1, 3 * d)),
            pl.BlockSpec((tm, d), lambda i: (i, 0)),
            weight(wa), weight(wb), weight(wc), weight(wo),
            const((1, d)),
        ],
        out_specs=pl.BlockSpec((tm, d), lambda i: (i, 0)),
        out_shape=jax.ShapeDtypeStruct((m, d), F32),
        compiler_params=_cparams(("parallel",), 56),
        name="merge",
    )(ya, yb, yc, p, p, p, p, p, p, p, p, bg.reshape(1, 3 * d), x2d, wa, wb, wc, wo, pg.reshape(1, d))


def _layer(x2d, layer, bsz, seq, pre_g, w_in, b_gate, q_g, k_g, na_bias, wa, wb, wc, wo, post_g, cos, sin,
           tables):
    p = _inproj(x2d, pre_g, w_in, layer, tm=min(1024, x2d.shape[0]))
    nprep = (GQ_HEADS + GKV_HEADS) // 2
    gains = jnp.concatenate([jnp.broadcast_to(q_g * QSCALE, (GQ_HEADS // 2, HEAD_DIM)),
                             jnp.broadcast_to(k_g, (GKV_HEADS // 2, HEAD_DIM))], axis=0)
    qk = _qkprep(p, gains.reshape(nprep, 1, HEAD_DIM), cos, sin, seq, tr=min(4096, seq))
    ya = _na(p, na_bias, layer, bsz, seq)
    yb = _gqa(qk, p, bsz, seq, tq=min(256, seq), tk=min(512, seq // GQA_UNROLL),
              ntiles=min(GQA_TILES, seq // min(256, seq)))
    yc = _dil(p, tables, bsz, seq)
    return _merge(ya, yb, yc, p, b_gate, x2d, wa, wb, wc, wo, post_g, layer, tm=256)


def kernel(x, pre_norm_g, w_in, b_gate, q_norm_g, k_norm_g, rpb, w_branch_a, w_branch_b, w_branch_c, w_out,
           post_norm_g):
    bsz, seq, d = x.shape
    depth = w_in.shape[0]
    cos, sin = _rope_tables(seq)
    tables = _dil_tables(seq)
    na_bias = _na_bias(rpb, seq)
    w_in, wa, wb, wc, wo = (w.astype(BF16) for w in (w_in, w_branch_a, w_branch_b, w_branch_c, w_out))
    x2d = x.reshape(bsz * seq, d)
    for l in range(depth):
        x2d = _layer(x2d, l, bsz, seq, pre_norm_g[l], w_in, b_gate[l], q_norm_g[l], k_norm_g[l], na_bias,
                     wa, wb, wc, wo, post_norm_g[l], cos, sin, tables)
    return x2d.reshape(bsz, seq, d)
```
